```python
import math
import jax
import jax.numpy as jnp
from jax import lax
import numpy as np

D_MODEL = 1024
BATCH = 4
SEQ = 4096
DEPTH = 2
DEC_BATCH = 128
DEC_SEQ = 1
PAST_LEN = 2048
PAGE_SIZE = 128

N_MIXERS = 4
N_HEADS = 4
HEAD_DIM = 64
MIX_W = N_HEADS * HEAD_DIM
REL_BUCKETS = 32
REL_MAX_DIST = 128
REL_HEADS = N_MIXERS * N_HEADS
MOBA_BLOCK = 256
MOBA_TOPK = 3
MOBA_QB = 32
NSA_CMP_LEN = 32
NSA_CMP_STRIDE = 16
NSA_CMP_HID = 128
NSA_SLC_BLOCK = 64
NSA_N_SEL = 16
NSA_WINDOW = 512
DIFF_DH = 32
DSA_IDX_HEADS = 4
DSA_IDX_DIM = 32
DSA_TOPK = 256
DSA_ROW = 2 * HEAD_DIM + DSA_IDX_DIM
PEER_HEADS = 8
PEER_KEYS = 128
PEER_EXPERTS = PEER_KEYS * PEER_KEYS
PEER_TOPK = 16
PEER_QDIM = 256
PEER_CHUNK = 128
Q_BLOCK = 128
EPS = 1e-6
NEG = -1e30
BIG = 1e9

IN_SPLITS = (
    ('moba_q', MIX_W), ('moba_k', MIX_W), ('moba_v', MIX_W),
    ('nsa_q', MIX_W), ('nsa_kc', HEAD_DIM), ('nsa_vc', HEAD_DIM), ('nsa_ks', HEAD_DIM),
    ('nsa_vs', HEAD_DIM), ('nsa_kw', HEAD_DIM), ('nsa_vw', HEAD_DIM), ('nsa_g', 3 * N_HEADS),
    ('diff_q', N_HEADS * 2 * DIFF_DH), ('diff_k', N_HEADS * 2 * DIFF_DH), ('diff_v', N_HEADS * 2 * DIFF_DH),
    ('dsa_q', MIX_W), ('dsa_k', HEAD_DIM), ('dsa_v', HEAD_DIM),
    ('dsa_qi', DSA_IDX_HEADS * DSA_IDX_DIM), ('dsa_ki', DSA_IDX_DIM), ('dsa_w', DSA_IDX_HEADS),
    ('gates', N_MIXERS * D_MODEL),
)
IN_COLS = sum(s for _, s in IN_SPLITS)

kernel_name = 'hybrid_moba_nsa_diff_dsa_peer_step'


def _rms(x, g):
    xf = x.astype(jnp.float32)
    return xf * lax.rsqrt(jnp.mean(xf * xf, axis=-1, keepdims=True) + EPS) * g.astype(jnp.float32)


def _masked_softmax(logits, mask):
    z = jnp.where(mask, logits.astype(jnp.float32), NEG)
    z = z - jnp.max(z, axis=-1, keepdims=True)
    p = jnp.where(mask, jnp.exp(z), 0.0)
    return p / jnp.maximum(jnp.sum(p, axis=-1, keepdims=True), 1e-30)


def _t5_bucket(rel):
    n = jnp.maximum(rel, 0)
    exact = REL_BUCKETS // 2
    nf = jnp.maximum(n, 1).astype(jnp.float32)
    large = exact + (jnp.log(nf / exact) / math.log(REL_MAX_DIST / exact) * (REL_BUCKETS - exact)).astype(jnp.int32)
    return jnp.where(n < exact, n, jnp.minimum(large, REL_BUCKETS - 1))


def _qblock_map(fn, qb, qpos, *qs):
    tq = qpos.shape[0]
    nb = tq // qb
    blocks = tuple(jnp.moveaxis(a.reshape((a.shape[0], nb, qb) + a.shape[2:]), 1, 0) for a in qs)
    out = lax.map(lambda args: fn(*args), (qpos.reshape(nb, qb),) + blocks)
    return jnp.moveaxis(out, 0, 1).reshape((out.shape[1], tq) + out.shape[3:])


def _split_in(p):
    offs = np.cumsum([s for _, s in IN_SPLITS])[:-1].tolist()
    parts = jnp.split(p, offs, axis=-1)
    return {name: a for (name, _), a in zip(IN_SPLITS, parts)}


def _last_rows(rows, n):
    t = rows.shape[1]
    if t >= n:
        return rows[:, t - n:]
    return jnp.pad(rows, ((0, 0), (n - t, 0)) + ((0, 0),) * (rows.ndim - 2))


def _moba(q, qpos, k, v, tab):
    b, tk = k.shape[0], k.shape[1]
    nblk = -(-tk // MOBA_BLOCK)
    pad = ((0, 0), (0, nblk * MOBA_BLOCK - tk), (0, 0), (0, 0))
    kb = jnp.pad(k, pad).reshape(b, nblk, MOBA_BLOCK, N_HEADS, HEAD_DIM)
    vb = jnp.pad(v, pad).reshape(b, nblk, MOBA_BLOCK, N_HEADS, HEAD_DIM)
    kmean = jnp.mean(kb.astype(jnp.float32), axis=2)
    kbh = jnp.transpose(kb, (0, 3, 1, 2, 4))
    vbh = jnp.transpose(vb, (0, 3, 1, 2, 4))
    ksel = min(MOBA_TOPK, nblk)
    nk = (ksel + 1) * MOBA_BLOCK
    bi = jnp.arange(b)[:, None, None, None]
    hi = jnp.arange(N_HEADS)[None, None, :, None]
    jj = jnp.arange(nblk)
    off = jnp.arange(MOBA_BLOCK)
    scale = HEAD_DIM ** -0.5

    def block_fn(pos, qq):
        qb = pos.shape[0]
        qf = qq.astype(jnp.float32)
        bt4 = (pos // MOBA_BLOCK)[None, :, None, None]
        gate = jnp.einsum('bthd,bjhd->bthj', qf, kmean)
        gate = jnp.where(jj[None, None, None, :] < bt4, gate, -BIG)
        _, top = lax.top_k(gate, ksel)
        own = jnp.broadcast_to(bt4, top.shape[:-1] + (1,))
        sel = jnp.concatenate([top, own], axis=-1)
        ok = jnp.concatenate([top < bt4, jnp.ones(own.shape, dtype=bool)], axis=-1)
        kg = kbh[bi, hi, sel].reshape(b, qb, N_HEADS, nk, HEAD_DIM)
        vg = vbh[bi, hi, sel].reshape(b, qb, N_HEADS, nk, HEAD_DIM)
        spos = (sel[..., None] * MOBA_BLOCK + off).reshape(b, qb, N_HEADS, nk)
        rel = pos[None, :, None, None] - spos
        mask = jnp.repeat(ok, MOBA_BLOCK, axis=-1) & (rel >= 0)
        logits = jnp.einsum('bthd,bthsd->bths', qf, kg) * scale + tab[_t5_bucket(rel), hi]
        p = _masked_softmax(logits, mask)
        return jnp.einsum('bths,bthsd->bthd', p, vg.astype(jnp.float32))

    return _qblock_map(block_fn, math.gcd(qpos.shape[0], MOBA_QB), qpos, q)


def _nsa(q, g, qpos, kc, vc, ks, vs, kw, vw, kpos0_w, pe, w1, w2, kn_cmp, tab):
    b, tk = kc.shape[0], kc.shape[1]
    n_cmp = (tk - NSA_CMP_LEN) // NSA_CMP_STRIDE + 1
    cstart = jnp.arange(n_cmp) * NSA_CMP_STRIDE
    idx = cstart[:, None] + jnp.arange(NSA_CMP_LEN)[None, :]

    def compress(rows, pe_i, w1_i, w2_i):
        blk = (rows[:, idx] + pe_i).reshape(b, n_cmp, NSA_CMP_LEN * HEAD_DIM)
        return jax.nn.gelu(blk @ w1_i) @ w2_i

    kcmp = _rms(compress(kc, pe[0], w1[0], w2[0]), kn_cmp)
    vcmp = compress(vc, pe[1], w1[1], w2[1])
    cend = cstart + NSA_CMP_LEN - 1
    n_slc = -(-tk // NSA_SLC_BLOCK)
    pad = ((0, 0), (0, n_slc * NSA_SLC_BLOCK - tk), (0, 0))
    ksb = jnp.pad(ks, pad).reshape(b, n_slc, NSA_SLC_BLOCK, HEAD_DIM)
    vsb = jnp.pad(vs, pad).reshape(b, n_slc, NSA_SLC_BLOCK, HEAD_DIM)
    jj = jnp.arange(n_slc)
    sstart = jj * NSA_SLC_BLOCK
    overlap = ((cstart[:, None] < sstart[None, :] + NSA_SLC_BLOCK)
               & (cstart[:, None] + NSA_CMP_LEN > sstart[None, :])).astype(jnp.float32)
    n_sel = min(NSA_N_SEL, n_slc)
    ns = n_sel * NSA_SLC_BLOCK
    kwp = jnp.pad(kw, ((0, 0), (NSA_WINDOW - 1, 0), (0, 0)))
    vwp = jnp.pad(vw, ((0, 0), (NSA_WINDOW - 1, 0), (0, 0)))
    bi = jnp.arange(b)[:, None, None]
    off = jnp.arange(NSA_SLC_BLOCK)
    scale = HEAD_DIM ** -0.5

    def block_fn(pos, qq, gg):
        qb = pos.shape[0]
        qf = qq.astype(jnp.float32)
        crel = pos[:, None] - cend[None, :]
        lc = jnp.einsum('bthd,bnd->bthn', qf, kcmp) * scale + jnp.transpose(tab[_t5_bucket(crel)], (0, 2, 1))[None]
        pc = _masked_softmax(lc, (crel >= 0)[None, :, None, :])
        oc = jnp.einsum('bthn,bnd->bthd', pc, vcmp)
        imp = jnp.einsum('bthn,nj->btj', pc, overlap)
        bt = pos // NSA_SLC_BLOCK
        forced = (jj[None, :] == 0) | (jj[None, :] == bt[:, None]) | (jj[None, :] == bt[:, None] - 1)
        adm = jj[None, :] <= bt[:, None]
        score = jnp.where(adm[None], jnp.where(forced[None], BIG, imp), -BIG)
        _, sel = lax.top_k(score, n_sel)
        kg = ksb[bi, sel].reshape(b, qb, ns, HEAD_DIM)
        vg = vsb[bi, sel].reshape(b, qb, ns, HEAD_DIM)
        spos = (sel[..., None] * NSA_SLC_BLOCK + off).reshape(b, qb, ns)
        srel = pos[None, :, None] - spos
        ls = jnp.einsum('bthd,btsd->bths', qf, kg) * scale + jnp.transpose(tab[_t5_bucket(srel)], (0, 1, 3, 2))
        ps = _masked_softmax(ls, (srel >= 0)[:, :, None, :])
        osel = jnp.einsum('bths,btsd->bthd', ps, vg.astype(jnp.float32))
        wlen = NSA_WINDOW - 1 + qb
        start = pos[0] - kpos0_w
        kwin = lax.dynamic_slice_in_dim(kwp, start, wlen, axis=1)
        vwin = lax.dynamic_slice_in_dim(vwp, start, wlen, axis=1)
        wpos = pos[0] - (NSA_WINDOW - 1) + jnp.arange(wlen)
        wrel = pos[:, None] - wpos[None, :]
        wmask = (wrel >= 0) & (wrel < NSA_WINDOW) & (wpos[None, :] >= kpos0_w)
        lw = jnp.einsum('bthd,bsd->bths', qf, kwin) * scale + jnp.transpose(tab[_t5_bucket(wrel)], (0, 2, 1))[None]
        pw = _masked_softmax(lw, wmask[None, :, None, :])
        ow = jnp.einsum('bths,bsd->bthd', pw, vwin.astype(jnp.float32))
        gf = gg.astype(jnp.float32)
        return gf[..., 0:1] * oc + gf[..., 1:2] * osel + gf[..., 2:3] * ow

    return _qblock_map(block_fn, math.gcd(qpos.shape[0], Q_BLOCK), qpos, q, g)


def _diff(q, qpos, k, v, lam, lam_init, out_gain, tab):
    tk = k.shape[1]
    kpos = jnp.arange(tk)
    scale = DIFF_DH ** -0.5

    def block_fn(pos, qq):
        rel = pos[:, None] - kpos[None, :]
        bias = jnp.transpose(tab[_t5_bucket(rel)], (2, 0, 1))
        logits = jnp.einsum('bthmd,bshmd->bmhts', qq.astype(jnp.float32), k.astype(jnp.float32)) * scale + bias
        p = _masked_softmax(logits, rel >= 0)
        a = p[:, 0] - lam * p[:, 1]
        o = jnp.einsum('bhts,bshd->bthd', a, v.astype(jnp.float32))
        return _rms(o, out_gain) * (1.0 - lam_init)

    return _qblock_map(block_fn, math.gcd(qpos.shape[0], Q_BLOCK), qpos, q)


def _dsa(q, qi, w, qpos, k, v, ki, tab):
    b, tk = k.shape[0], k.shape[1]
    kpos = jnp.arange(tk)
    ksel = min(DSA_TOPK, tk // 4)
    bi = jnp.arange(b)[:, None, None]
    scale = HEAD_DIM ** -0.5

    def block_fn(pos, qq, qqi, ww):
        idx = jnp.einsum('bthd,bsd->bths', qqi.astype(jnp.float32), ki.astype(jnp.float32))
        score = jnp.einsum('bths,bth->bts', jax.nn.relu(idx), ww.astype(jnp.float32))
        score = jnp.where((kpos[None, :] <= pos[:, None])[None], score, -BIG)
        _, sel = lax.top_k(score, ksel)
        kg = k[bi, sel]
        vg = v[bi, sel]
        rel = pos[None, :, None] - sel
        logits = jnp.einsum('bthd,btsd->bths', qq.astype(jnp.float32), kg.astype(jnp.float32)) * scale \
            + jnp.transpose(tab[_t5_bucket(rel)], (0, 1, 3, 2))
        p = _masked_softmax(logits, (rel >= 0)[:, :, None, :])
        return jnp.einsum('bths,btsd->bthd', p, vg.astype(jnp.float32))

    return _qblock_map(block_fn, math.gcd(qpos.shape[0], Q_BLOCK), qpos, q, qi, w)


def _peer(h, wq, subkeys, u, v):
    n = h.shape[0]
    c = math.gcd(n, PEER_CHUNK)

    def chunk_fn(hc):
        qh = (hc @ wq).reshape(c, PEER_HEADS, 2, PEER_QDIM // 2)
        s1 = jnp.einsum('nhd,hkd->nhk', qh[:, :, 0], subkeys[0])
        s2 = jnp.einsum('nhd,hkd->nhk', qh[:, :, 1], subkeys[1])
        t1, i1 = lax.top_k(s1, PEER_TOPK)
        t2, i2 = lax.top_k(s2, PEER_TOPK)
        cand = (t1[..., :, None] + t2[..., None, :]).reshape(c, PEER_HEADS, PEER_TOPK * PEER_TOPK)
        cidx = (i1[..., :, None] * PEER_KEYS + i2[..., None, :]).reshape(c, PEER_HEADS, PEER_TOPK * PEER_TOPK)
        ts, ti = lax.top_k(cand, PEER_TOPK)
        eidx = jnp.take_along_axis(cidx, ti, axis=-1)
        gate = jax.nn.softmax(ts.astype(jnp.float32), axis=-1)
        act = jax.nn.gelu(jnp.einsum('nd,nhkd->nhk', hc, u[eidx]).astype(jnp.float32))
        return jnp.einsum('nhk,nhkd->nd', gate * act, v[eidx].astype(jnp.float32))

    out = lax.map(chunk_fn, h.reshape(n // c, c, h.shape[-1]))
    return out.reshape(n, -1)


def _layer(x, qpos, past, win_past, kpos0_w, win_buf, lp, rel_bias, lam_init):
    b, t, _ = x.shape
    hn = _rms(x, lp['norm_mix'])
    pr = _split_in(hn @ lp['w_in'])
    qk = lp['qk_gain']
    mq = _rms(pr['moba_q'].reshape(b, t, N_HEADS, HEAD_DIM), qk[0])
    mk = _rms(pr['moba_k'].reshape(b, t, N_HEADS, HEAD_DIM), qk[1])
    mv = pr['moba_v'].reshape(b, t, N_HEADS, HEAD_DIM).astype(jnp.float32)
    moba_new = jnp.stack([mk, mv], axis=2).astype(x.dtype)
    nq = _rms(pr['nsa_q'].reshape(b, t, N_HEADS, HEAD_DIM), qk[2])
    nsa_new = jnp.stack([pr['nsa_kc'].astype(jnp.float32), pr['nsa_vc'].astype(jnp.float32),
                         _rms(pr['nsa_ks'], qk[4]), pr['nsa_vs'].astype(jnp.float32)], axis=2).astype(x.dtype)
    win_new = jnp.stack([_rms(pr['nsa_kw'], qk[5]), pr['nsa_vw'].astype(jnp.float32)], axis=2).astype(x.dtype)
    ng = jax.nn.sigmoid(pr['nsa_g'].reshape(b, t, N_HEADS, 3).astype(jnp.float32))
    dq = _rms(pr['diff_q'].reshape(b, t, N_HEADS, 2, DIFF_DH), lp['diff_qk_gain'][0])
    dk = _rms(pr['diff_k'].reshape(b, t, N_HEADS, 2, DIFF_DH), lp['diff_qk_gain'][1])
    diff_new = jnp.stack([dk.reshape(b, t, N_HEADS, 2 * DIFF_DH),
                          pr['diff_v'].reshape(b, t, N_HEADS, 2 * DIFF_DH).astype(jnp.float32)], axis=2).astype(x.dtype)
    sq = _rms(pr['dsa_q'].reshape(b, t, N_HEADS, HEAD_DIM), qk[6])
    dsa_new = jnp.concatenate([_rms(pr['dsa_k'], qk[7]), pr['dsa_v'].astype(jnp.float32),
                               pr['dsa_ki'].astype(jnp.float32)], axis=-1).astype(x.dtype)
    sqi = pr['dsa_qi'].reshape(b, t, DSA_IDX_HEADS, DSA_IDX_DIM)
    sw = pr['dsa_w']
    if past is None:
        f_moba, f_nsa, f_diff, f_dsa = moba_new, nsa_new, diff_new, dsa_new
        wrows = win_new
    else:
        f_moba = jnp.concatenate([past[0], moba_new], axis=1)
        f_nsa = jnp.concatenate([past[1], nsa_new], axis=1)
        f_diff = jnp.concatenate([past[2], diff_new], axis=1)
        f_dsa = jnp.concatenate([past[3], dsa_new], axis=1)
        wrows = jnp.concatenate([win_past, win_new], axis=1)
    tk = f_moba.shape[1]
    o_a = _moba(mq, qpos, f_moba[:, :, 0], f_moba[:, :, 1], rel_bias[:, 0:4])
    o_b = _nsa(nq, ng, qpos, f_nsa[:, :, 0], f_nsa[:, :, 1], f_nsa[:, :, 2], f_nsa[:, :, 3],
               wrows[:, :, 0], wrows[:, :, 1], kpos0_w, lp['nsa_pe'], lp['nsa_w1'], lp['nsa_w2'], qk[3],
               rel_bias[:, 4:8])
    dl = lp['diff_lambda'].astype(jnp.float32)
    lam = jnp.exp(jnp.sum(dl[0] * dl[1])) - jnp.exp(jnp.sum(dl[2] * dl[3])) + lam_init
    o_c = _diff(dq, qpos, f_diff[:, :, 0].reshape(b, tk, N_HEADS, 2, DIFF_DH), f_diff[:, :, 1],
                lam, lam_init, lp['diff_out_gain'], rel_bias[:, 8:12])
    o_d = _dsa(sq, sqi, sw, qpos, f_dsa[..., :HEAD_DIM], f_dsa[..., HEAD_DIM:2 * HEAD_DIM],
               f_dsa[..., 2 * HEAD_DIM:], rel_bias[:, 12:16])
    br = jnp.stack([o_a.reshape(b, t, MIX_W), o_b.reshape(b, t, MIX_W),
                    o_c.reshape(b, t, MIX_W), o_d.reshape(b, t, MIX_W)], axis=2)
    z = jnp.einsum('btim,ime->btie', br, lp['w_branch'])
    gates = jax.nn.sigmoid(pr['gates'].reshape(b, t, N_MIXERS, D_MODEL).astype(jnp.float32))
    x = x + (jnp.sum(gates * z, axis=2) @ lp['w_out']).astype(x.dtype)
    hf = _rms(x, lp['norm_ffn'])
    ff = _peer(hf.reshape(b * t, D_MODEL), lp['peer_wq'], lp['peer_subkeys'], lp['peer_u'], lp['peer_v'])
    x = x + ff.reshape(b, t, D_MODEL).astype(x.dtype)
    return x, moba_new, nsa_new, diff_new, dsa_new, _last_rows(wrows, win_buf)


def setup_inputs(seed: int = 0) -> dict:
    key = jax.random.key(seed)
    ks = iter(jax.random.split(key, 40))

    def nrm(shape, s):
        return jax.random.normal(next(ks), shape, jnp.float32) * s

    n_pages = PAST_LEN // PAGE_SIZE
    used = DEC_BATCH * n_pages
    n_pool = used + used // 4
    win_buf = min(NSA_WINDOW, PAST_LEN)
    page_table = jax.random.permutation(next(ks), n_pool)[:used].reshape(DEC_BATCH, n_pages).astype(jnp.int32)
    return {
        'x_prompt': nrm((BATCH, SEQ, D_MODEL), 1.0),
        'x_sample': nrm((DEC_BATCH, DEC_SEQ, D_MODEL), 1.0),
        'cache_moba': nrm((DEPTH, n_pool, PAGE_SIZE, 2, N_HEADS, HEAD_DIM), 1.0),
        'cache_nsa': nrm((DEPTH, n_pool, PAGE_SIZE, 4, HEAD_DIM), 1.0),
        'cache_diff': nrm((DEPTH, n_pool, PAGE_SIZE, 2, N_HEADS, 2 * DIFF_DH), 1.0),
        'cache_dsa': nrm((DEPTH, n_pool, PAGE_SIZE, DSA_ROW), 1.0),
        'state_nsa_win': nrm((DEPTH, DEC_BATCH, win_buf, 2, HEAD_DIM), 1.0),
        'page_table': page_table,
        'rel_bias': nrm((REL_BUCKETS, REL_HEADS), 0.5),
        'norm_mix': 1.0 + nrm((DEPTH, D_MODEL), 0.02),
        'w_in': nrm((DEPTH, D_MODEL, IN_COLS), D_MODEL ** -0.5),
        'qk_gain': 1.0 + nrm((DEPTH, 8, HEAD_DIM), 0.02),
        'nsa_pe': nrm((DEPTH, 2, NSA_CMP_LEN, HEAD_DIM), 0.1),
        'nsa_w1': nrm((DEPTH, 2, NSA_CMP_LEN * HEAD_DIM, NSA_CMP_HID), (NSA_CMP_LEN * HEAD_DIM) ** -0.5),
        'nsa_w2': nrm((DEPTH, 2, NSA_CMP_HID, HEAD_DIM), NSA_CMP_HID ** -0.5),
        'diff_qk_gain': 1.0 + nrm((DEPTH, 2, DIFF_DH), 0.02),
        'diff_lambda': nrm((DEPTH, 4, DIFF_DH), 0.1),
        'diff_out_gain': 1.0 + nrm((DEPTH, 2 * DIFF_DH), 0.02),
        'w_branch': nrm((DEPTH, N_MIXERS, MIX_W, D_MODEL), MIX_W ** -0.5),
        'w_out': nrm((DEPTH, D_MODEL, D_MODEL), D_MODEL ** -0.5),
        'norm_ffn': 1.0 + nrm((DEPTH, D_MODEL), 0.02),
        'peer_wq': nrm((DEPTH, D_MODEL, PEER_HEADS * PEER_QDIM), D_MODEL ** -0.5),
        'peer_subkeys': nrm((DEPTH, 2, PEER_HEADS, PEER_KEYS, PEER_QDIM // 2), (PEER_QDIM // 2) ** -0.5),
        'peer_u': nrm((DEPTH, PEER_EXPERTS, D_MODEL), D_MODEL ** -0.5),
        'peer_v': nrm((DEPTH, PEER_EXPERTS, D_MODEL), 0.25),
    }


def reference(x_prompt, x_sample, cache_moba, cache_nsa, cache_diff, cache_dsa, state_nsa_win, page_table,
              rel_bias, norm_mix, w_in, qk_gain, nsa_pe, nsa_w1, nsa_w2, diff_qk_gain, diff_lambda,
              diff_out_gain, w_branch, w_out, norm_ffn, peer_wq, peer_subkeys, peer_u, peer_v):
    past_len = page_table.shape[1] * cache_moba.shape[2]
    win_buf = state_nsa_win.shape[2]
    qpos_p = jnp.arange(x_prompt.shape[1], dtype=jnp.int32)
    qpos_s = past_len + jnp.arange(x_sample.shape[1], dtype=jnp.int32)

    def paged(cache_l):
        g = cache_l[page_table]
        return g.reshape((g.shape[0], g.shape[1] * g.shape[2]) + g.shape[3:])

    yp, ys = x_prompt, x_sample
    rows_p, rows_s = [], []
    for l in range(DEPTH):
        lp = {'norm_mix': norm_mix[l], 'w_in': w_in[l], 'qk_gain': qk_gain[l], 'nsa_pe': nsa_pe[l],
              'nsa_w1': nsa_w1[l], 'nsa_w2': nsa_w2[l], 'diff_qk_gain': diff_qk_gain[l],
              'diff_lambda': diff_lambda[l], 'diff_out_gain': diff_out_gain[l], 'w_branch': w_branch[l],
              'w_out': w_out[l], 'norm_ffn': norm_ffn[l], 'peer_wq': peer_wq[l],
              'peer_subkeys': peer_subkeys[l], 'peer_u': peer_u[l], 'peer_v': peer_v[l]}
        lam_init = 0.8 - 0.6 * math.exp(-0.3 * l)
        out_p = _layer(yp, qpos_p, None, None, 0, win_buf, lp, rel_bias, lam_init)
        past = (paged(cache_moba[l]), paged(cache_nsa[l]), paged(cache_diff[l]), paged(cache_dsa[l]))
        out_s = _layer(ys, qpos_s, past, state_nsa_win[l], past_len - win_buf, win_buf, lp, rel_bias, lam_init)
        yp, ys = out_p[0], out_s[0]
        rows_p.append(out_p[1:])
        rows_s.append(out_s[1:])

    def stk(rows, i):
        return jnp.stack([r[i] for r in rows], axis=0)

    return (yp, ys, stk(rows_p, 0), stk(rows_s, 0), stk(rows_p, 1), stk(rows_s, 1),
            stk(rows_p, 2), stk(rows_s, 2), stk(rows_p, 3), stk(rows_s, 3), stk(rows_p, 4), stk(rows_s, 4))
```

```python
import functools
import math

import jax
import jax.numpy as jnp
import numpy as np
from jax import lax
from jax.experimental import pallas as pl
from jax.experimental.pallas import tpu as pltpu

D_MODEL = 1024
N_MIXERS = 4
N_HEADS = 4
HEAD_DIM = 64
MIX_W = N_HEADS * HEAD_DIM
REL_BUCKETS = 32
REL_MAX_DIST = 128
MOBA_BLOCK = 256
MOBA_TOPK = 3
MOBA_QB = 32
NSA_CMP_LEN = 32
NSA_CMP_STRIDE = 16
NSA_SLC_BLOCK = 64
NSA_N_SEL = 16
NSA_WINDOW = 512
DIFF_DH = 32
DSA_IDX_HEADS = 4
DSA_IDX_DIM = 32
DSA_TOPK = 256
PEER_HEADS = 8
PEER_KEYS = 128
PEER_TOPK = 16
PEER_QDIM = 256
PEER_CHUNK = 128
Q_BLOCK = 128
EPS = 1e-6
NEG = -1e30
BIG = 1e9

IN_SPLITS = (
    ('moba_q', MIX_W), ('moba_k', MIX_W), ('moba_v', MIX_W),
    ('nsa_q', MIX_W), ('nsa_kc', HEAD_DIM), ('nsa_vc', HEAD_DIM), ('nsa_ks', HEAD_DIM),
    ('nsa_vs', HEAD_DIM), ('nsa_kw', HEAD_DIM), ('nsa_vw', HEAD_DIM), ('nsa_g', 3 * N_HEADS),
    ('diff_q', N_HEADS * 2 * DIFF_DH), ('diff_k', N_HEADS * 2 * DIFF_DH), ('diff_v', N_HEADS * 2 * DIFF_DH),
    ('dsa_q', MIX_W), ('dsa_k', HEAD_DIM), ('dsa_v', HEAD_DIM),
    ('dsa_qi', DSA_IDX_HEADS * DSA_IDX_DIM), ('dsa_ki', DSA_IDX_DIM), ('dsa_w', DSA_IDX_HEADS),
    ('gates', N_MIXERS * D_MODEL),
)
IN_COLS = sum(s for _, s in IN_SPLITS)

LANE = 128


def _norm_matmul_kernel(x_ref, g_ref, w_ref, o_ref):
    x = x_ref[...]
    hn = x * lax.rsqrt(jnp.mean(x * x, axis=-1, keepdims=True) + EPS) * g_ref[...]
    o_ref[...] = jnp.dot(hn.astype(jnp.bfloat16), w_ref[...], preferred_element_type=jnp.float32)


def _norm_matmul(x, g, w_bf16, tm, tn):
    m, k = x.shape
    n = w_bf16.shape[1]
    return pl.pallas_call(
        _norm_matmul_kernel,
        grid=(n // tn, m // tm),
        in_specs=[pl.BlockSpec((tm, k), lambda j, i: (i, 0)),
                  pl.BlockSpec((1, k), lambda j, i: (0, 0)),
                  pl.BlockSpec((k, tn), lambda j, i: (0, j))],
        out_specs=pl.BlockSpec((tm, tn), lambda j, i: (i, j)),
        out_shape=jax.ShapeDtypeStruct((m, n), jnp.float32),
        compiler_params=pltpu.CompilerParams(dimension_semantics=("arbitrary", "arbitrary")),
    )(x, g.reshape(1, k), w_bf16)


def _in_proj(x2d, g, w_in):
    m = x2d.shape[0]
    tn = 512
    n_pad = -(-IN_COLS // tn) * tn
    w = jnp.pad(w_in, ((0, 0), (0, n_pad - IN_COLS))).astype(jnp.bfloat16)
    tm = math.gcd(m, 512)
    return _norm_matmul(x2d, g, w, tm, tn)[:, :IN_COLS]


def _rms(x, g):
    xf = x.astype(jnp.float32)
    return xf * lax.rsqrt(jnp.mean(xf * xf, axis=-1, keepdims=True) + EPS) * g.astype(jnp.float32)


def _masked_softmax(logits, mask):
    z = jnp.where(mask, logits.astype(jnp.float32), NEG)
    z = z - jnp.max(z, axis=-1, keepdims=True)
    p = jnp.where(mask, jnp.exp(z), 0.0)
    return p / jnp.maximum(jnp.sum(p, axis=-1, keepdims=True), 1e-30)


def _t5_bucket(rel):
    n = jnp.maximum(rel, 0)
    exact = REL_BUCKETS // 2
    nf = jnp.maximum(n, 1).astype(jnp.float32)
    large = exact + (jnp.log(nf / exact) / math.log(REL_MAX_DIST / exact) * (REL_BUCKETS - exact)).astype(jnp.int32)
    return jnp.where(n < exact, n, jnp.minimum(large, REL_BUCKETS - 1))


def _qblock_map(fn, qb, qpos, *qs):
    tq = qpos.shape[0]
    nb = tq // qb
    blocks = tuple(jnp.moveaxis(a.reshape((a.shape[0], nb, qb) + a.shape[2:]), 1, 0) for a in qs)
    out = lax.map(lambda args: fn(*args), (qpos.reshape(nb, qb),) + blocks)
    return jnp.moveaxis(out, 0, 1).reshape((out.shape[1], tq) + out.shape[3:])


def _split_in(p):
    offs = np.cumsum([s for _, s in IN_SPLITS])[:-1].tolist()
    parts = jnp.split(p, offs, axis=-1)
    return {name: a for (name, _), a in zip(IN_SPLITS, parts)}


def _last_rows(rows, n):
    t = rows.shape[1]
    if t >= n:
        return rows[:, t - n:]
    return jnp.pad(rows, ((0, 0), (n - t, 0)) + ((0, 0),) * (rows.ndim - 2))


def _moba(q, qpos, k, v, tab):
    b, tk = k.shape[0], k.shape[1]
    nblk = -(-tk // MOBA_BLOCK)
    pad = ((0, 0), (0, nblk * MOBA_BLOCK - tk), (0, 0), (0, 0))
    kb = jnp.pad(k, pad).reshape(b, nblk, MOBA_BLOCK, N_HEADS, HEAD_DIM)
    vb = jnp.pad(v, pad).reshape(b, nblk, MOBA_BLOCK, N_HEADS, HEAD_DIM)
    kmean = jnp.mean(kb.astype(jnp.float32), axis=2)
    kbh = jnp.transpose(kb, (0, 3, 1, 2, 4))
    vbh = jnp.transpose(vb, (0, 3, 1, 2, 4))
    ksel = min(MOBA_TOPK, nblk)
    nk = (ksel + 1) * MOBA_BLOCK
    bi = jnp.arange(b)[:, None, None, None]
    hi = jnp.arange(N_HEADS)[None, None, :, None]
    jj = jnp.arange(nblk)
    off = jnp.arange(MOBA_BLOCK)
    scale = HEAD_DIM ** -0.5

    def block_fn(pos, qq):
        qb = pos.shape[0]
        qf = qq.astype(jnp.float32)
        bt4 = (pos // MOBA_BLOCK)[None, :, None, None]
        gate = jnp.einsum('bthd,bjhd->bthj', qf, kmean)
        gate = jnp.where(jj[None, None, None, :] < bt4, gate, -BIG)
        _, top = lax.top_k(gate, ksel)
        own = jnp.broadcast_to(bt4, top.shape[:-1] + (1,))
        sel = jnp.concatenate([top, own], axis=-1)
        ok = jnp.concatenate([top < bt4, jnp.ones(own.shape, dtype=bool)], axis=-1)
        kg = kbh[bi, hi, sel].reshape(b, qb, N_HEADS, nk, HEAD_DIM)
        vg = vbh[bi, hi, sel].reshape(b, qb, N_HEADS, nk, HEAD_DIM)
        spos = (sel[..., None] * MOBA_BLOCK + off).reshape(b, qb, N_HEADS, nk)
        rel = pos[None, :, None, None] - spos
        mask = jnp.repeat(ok, MOBA_BLOCK, axis=-1) & (rel >= 0)
        logits = jnp.einsum('bthd,bthsd->bths', qf, kg) * scale + tab[_t5_bucket(rel), hi]
        p = _masked_softmax(logits, mask)
        return jnp.einsum('bths,bthsd->bthd', p, vg.astype(jnp.float32))

    return _qblock_map(block_fn, math.gcd(qpos.shape[0], MOBA_QB), qpos, q)


def _nsa(q, g, qpos, kc, vc, ks, vs, kw, vw, kpos0_w, pe, w1, w2, kn_cmp, tab):
    b, tk = kc.shape[0], kc.shape[1]
    n_cmp = (tk - NSA_CMP_LEN) // NSA_CMP_STRIDE + 1
    cstart = jnp.arange(n_cmp) * NSA_CMP_STRIDE
    idx = cstart[:, None] + jnp.arange(NSA_CMP_LEN)[None, :]

    def compress(rows, pe_i, w1_i, w2_i):
        blk = (rows[:, idx] + pe_i).reshape(b, n_cmp, NSA_CMP_LEN * HEAD_DIM)
        return jax.nn.gelu(blk @ w1_i) @ w2_i

    kcmp = _rms(compress(kc, pe[0], w1[0], w2[0]), kn_cmp)
    vcmp = compress(vc, pe[1], w1[1], w2[1])
    cend = cstart + NSA_CMP_LEN - 1
    n_slc = -(-tk // NSA_SLC_BLOCK)
    pad = ((0, 0), (0, n_slc * NSA_SLC_BLOCK - tk), (0, 0))
    ksb = jnp.pad(ks, pad).reshape(b, n_slc, NSA_SLC_BLOCK, HEAD_DIM)
    vsb = jnp.pad(vs, pad).reshape(b, n_slc, NSA_SLC_BLOCK, HEAD_DIM)
    jj = jnp.arange(n_slc)
    sstart = jj * NSA_SLC_BLOCK
    overlap = ((cstart[:, None] < sstart[None, :] + NSA_SLC_BLOCK)
               & (cstart[:, None] + NSA_CMP_LEN > sstart[None, :])).astype(jnp.float32)
    n_sel = min(NSA_N_SEL, n_slc)
    ns = n_sel * NSA_SLC_BLOCK
    kwp = jnp.pad(kw, ((0, 0), (NSA_WINDOW - 1, 0), (0, 0)))
    vwp = jnp.pad(vw, ((0, 0), (NSA_WINDOW - 1, 0), (0, 0)))
    bi = jnp.arange(b)[:, None, None]
    off = jnp.arange(NSA_SLC_BLOCK)
    scale = HEAD_DIM ** -0.5

    def block_fn(pos, qq, gg):
        qb = pos.shape[0]
        qf = qq.astype(jnp.float32)
        crel = pos[:, None] - cend[None, :]
        lc = jnp.einsum('bthd,bnd->bthn', qf, kcmp) * scale + jnp.transpose(tab[_t5_bucket(crel)], (0, 2, 1))[None]
        pc = _masked_softmax(lc, (crel >= 0)[None, :, None, :])
        oc = jnp.einsum('bthn,bnd->bthd', pc, vcmp)
        imp = jnp.einsum('bthn,nj->btj', pc, overlap)
        bt = pos // NSA_SLC_BLOCK
        forced = (jj[None, :] == 0) | (jj[None, :] == bt[:, None]) | (jj[None, :] == bt[:, None] - 1)
        adm = jj[None, :] <= bt[:, None]
        score = jnp.where(adm[None], jnp.where(forced[None], BIG, imp), -BIG)
        _, sel = lax.top_k(score, n_sel)
        kg = ksb[bi, sel].reshape(b, qb, ns, HEAD_DIM)
        vg = vsb[bi, sel].reshape(b, qb, ns, HEAD_DIM)
        spos = (sel[..., None] * NSA_SLC_BLOCK + off).reshape(b, qb, ns)
        srel = pos[None, :, None] - spos
        ls = jnp.einsum('bthd,btsd->bths', qf, kg) * scale + jnp.transpose(tab[_t5_bucket(srel)], (0, 1, 3, 2))
        ps = _masked_softmax(ls, (srel >= 0)[:, :, None, :])
        osel = jnp.einsum('bths,btsd->bthd', ps, vg.astype(jnp.float32))
        wlen = NSA_WINDOW - 1 + qb
        start = pos[0] - kpos0_w
        kwin = lax.dynamic_slice_in_dim(kwp, start, wlen, axis=1)
        vwin = lax.dynamic_slice_in_dim(vwp, start, wlen, axis=1)
        wpos = pos[0] - (NSA_WINDOW - 1) + jnp.arange(wlen)
        wrel = pos[:, None] - wpos[None, :]
        wmask = (wrel >= 0) & (wrel < NSA_WINDOW) & (wpos[None, :] >= kpos0_w)
        lw = jnp.einsum('bthd,bsd->bths', qf, kwin) * scale + jnp.transpose(tab[_t5_bucket(wrel)], (0, 2, 1))[None]
        pw = _masked_softmax(lw, wmask[None, :, None, :])
        ow = jnp.einsum('bths,bsd->bthd', pw, vwin.astype(jnp.float32))
        gf = gg.astype(jnp.float32)
        return gf[..., 0:1] * oc + gf[..., 1:2] * osel + gf[..., 2:3] * ow

    return _qblock_map(block_fn, math.gcd(qpos.shape[0], Q_BLOCK), qpos, q, g)


def _diff(q, qpos, k, v, lam, lam_init, out_gain, tab):
    tk = k.shape[1]
    kpos = jnp.arange(tk)
    scale = DIFF_DH ** -0.5

    def block_fn(pos, qq):
        rel = pos[:, None] - kpos[None, :]
        bias = jnp.transpose(tab[_t5_bucket(rel)], (2, 0, 1))
        logits = jnp.einsum('bthmd,bshmd->bmhts', qq.astype(jnp.float32), k.astype(jnp.float32)) * scale + bias
        p = _masked_softmax(logits, rel >= 0)
        a = p[:, 0] - lam * p[:, 1]
        o = jnp.einsum('bhts,bshd->bthd', a, v.astype(jnp.float32))
        return _rms(o, out_gain) * (1.0 - lam_init)

    return _qblock_map(block_fn, math.gcd(qpos.shape[0], Q_BLOCK), qpos, q)


def _dsa(q, qi, w, qpos, k, v, ki, tab):
    b, tk = k.shape[0], k.shape[1]
    kpos = jnp.arange(tk)
    ksel = min(DSA_TOPK, tk // 4)
    bi = jnp.arange(b)[:, None, None]
    scale = HEAD_DIM ** -0.5

    def block_fn(pos, qq, qqi, ww):
        idx = jnp.einsum('bthd,bsd->bths', qqi.astype(jnp.float32), ki.astype(jnp.float32))
        score = jnp.einsum('bths,bth->bts', jax.nn.relu(idx), ww.astype(jnp.float32))
        score = jnp.where((kpos[None, :] <= pos[:, None])[None], score, -BIG)
        _, sel = lax.top_k(score, ksel)
        kg = k[bi, sel]
        vg = v[bi, sel]
        rel = pos[None, :, None] - sel
        logits = jnp.einsum('bthd,btsd->bths', qq.astype(jnp.float32), kg.astype(jnp.float32)) * scale \
            + jnp.transpose(tab[_t5_bucket(rel)], (0, 1, 3, 2))
        p = _masked_softmax(logits, (rel >= 0)[:, :, None, :])
        return jnp.einsum('bths,btsd->bthd', p, vg.astype(jnp.float32))

    return _qblock_map(block_fn, math.gcd(qpos.shape[0], Q_BLOCK), qpos, q, qi, w)


def _peer(h, wq, subkeys, u, v):
    n = h.shape[0]
    c = math.gcd(n, PEER_CHUNK)

    def chunk_fn(hc):
        qh = (hc @ wq).reshape(c, PEER_HEADS, 2, PEER_QDIM // 2)
        s1 = jnp.einsum('nhd,hkd->nhk', qh[:, :, 0], subkeys[0])
        s2 = jnp.einsum('nhd,hkd->nhk', qh[:, :, 1], subkeys[1])
        t1, i1 = lax.top_k(s1, PEER_TOPK)
        t2, i2 = lax.top_k(s2, PEER_TOPK)
        cand = (t1[..., :, None] + t2[..., None, :]).reshape(c, PEER_HEADS, PEER_TOPK * PEER_TOPK)
        cidx = (i1[..., :, None] * PEER_KEYS + i2[..., None, :]).reshape(c, PEER_HEADS, PEER_TOPK * PEER_TOPK)
        ts, ti = lax.top_k(cand, PEER_TOPK)
        eidx = jnp.take_along_axis(cidx, ti, axis=-1)
        gate = jax.nn.softmax(ts.astype(jnp.float32), axis=-1)
        act = jax.nn.gelu(jnp.einsum('nd,nhkd->nhk', hc, u[eidx]).astype(jnp.float32))
        return jnp.einsum('nhk,nhkd->nd', gate * act, v[eidx].astype(jnp.float32))

    out = lax.map(chunk_fn, h.reshape(n // c, c, h.shape[-1]))
    return out.reshape(n, -1)


def _layer(x, qpos, past, win_past, kpos0_w, win_buf, lp, rel_bias, lam_init):
    b, t, _ = x.shape
    hn = _rms(x, lp['norm_mix'])
    pr = _split_in(_in_proj(x.reshape(b * t, D_MODEL), lp['norm_mix'], lp['w_in']).reshape(b, t, IN_COLS))
    qk = lp['qk_gain']
    mq = _rms(pr['moba_q'].reshape(b, t, N_HEADS, HEAD_DIM), qk[0])
    mk = _rms(pr['moba_k'].reshape(b, t, N_HEADS, HEAD_DIM), qk[1])
    mv = pr['moba_v'].reshape(b, t, N_HEADS, HEAD_DIM).astype(jnp.float32)
    moba_new = jnp.stack([mk, mv], axis=2).astype(x.dtype)
    nq = _rms(pr['nsa_q'].reshape(b, t, N_HEADS, HEAD_DIM), qk[2])
    nsa_new = jnp.stack([pr['nsa_kc'].astype(jnp.float32), pr['nsa_vc'].astype(jnp.float32),
                         _rms(pr['nsa_ks'], qk[4]), pr['nsa_vs'].astype(jnp.float32)], axis=2).astype(x.dtype)
    win_new = jnp.stack([_rms(pr['nsa_kw'], qk[5]), pr['nsa_vw'].astype(jnp.float32)], axis=2).astype(x.dtype)
    ng = jax.nn.sigmoid(pr['nsa_g'].reshape(b, t, N_HEADS, 3).astype(jnp.float32))
    dq = _rms(pr['diff_q'].reshape(b, t, N_HEADS, 2, DIFF_DH), lp['diff_qk_gain'][0])
    dk = _rms(pr['diff_k'].reshape(b, t, N_HEADS, 2, DIFF_DH), lp['diff_qk_gain'][1])
    diff_new = jnp.stack([dk.reshape(b, t, N_HEADS, 2 * DIFF_DH),
                          pr['diff_v'].reshape(b, t, N_HEADS, 2 * DIFF_DH).astype(jnp.float32)], axis=2).astype(x.dtype)
    sq = _rms(pr['dsa_q'].reshape(b, t, N_HEADS, HEAD_DIM), qk[6])
    dsa_new = jnp.concatenate([_rms(pr['dsa_k'], qk[7]), pr['dsa_v'].astype(jnp.float32),
                               pr['dsa_ki'].astype(jnp.float32)], axis=-1).astype(x.dtype)
    sqi = pr['dsa_qi'].reshape(b, t, DSA_IDX_HEADS, DSA_IDX_DIM)
    sw = pr['dsa_w']
    if past is None:
        f_moba, f_nsa, f_diff, f_dsa = moba_new, nsa_new, diff_new, dsa_new
        wrows = win_new
    else:
        f_moba = jnp.concatenate([past[0], moba_new], axis=1)
        f_nsa = jnp.concatenate([past[1], nsa_new], axis=1)
        f_diff = jnp.concatenate([past[2], diff_new], axis=1)
        f_dsa = jnp.concatenate([past[3], dsa_new], axis=1)
        wrows = jnp.concatenate([win_past, win_new], axis=1)
    tk = f_moba.shape[1]
    o_a = _moba(mq, qpos, f_moba[:, :, 0], f_moba[:, :, 1], rel_bias[:, 0:4])
    o_b = _nsa(nq, ng, qpos, f_nsa[:, :, 0], f_nsa[:, :, 1], f_nsa[:, :, 2], f_nsa[:, :, 3],
               wrows[:, :, 0], wrows[:, :, 1], kpos0_w, lp['nsa_pe'], lp['nsa_w1'], lp['nsa_w2'], qk[3],
               rel_bias[:, 4:8])
    dl = lp['diff_lambda'].astype(jnp.float32)
    lam = jnp.exp(jnp.sum(dl[0] * dl[1])) - jnp.exp(jnp.sum(dl[2] * dl[3])) + lam_init
    o_c = _diff(dq, qpos, f_diff[:, :, 0].reshape(b, tk, N_HEADS, 2, DIFF_DH), f_diff[:, :, 1],
                lam, lam_init, lp['diff_out_gain'], rel_bias[:, 8:12])
    o_d = _dsa(sq, sqi, sw, qpos, f_dsa[..., :HEAD_DIM], f_dsa[..., HEAD_DIM:2 * HEAD_DIM],
               f_dsa[..., 2 * HEAD_DIM:], rel_bias[:, 12:16])
    br = jnp.stack([o_a.reshape(b, t, MIX_W), o_b.reshape(b, t, MIX_W),
                    o_c.reshape(b, t, MIX_W), o_d.reshape(b, t, MIX_W)], axis=2)
    z = jnp.einsum('btim,ime->btie', br, lp['w_branch'])
    gates = jax.nn.sigmoid(pr['gates'].reshape(b, t, N_MIXERS, D_MODEL).astype(jnp.float32))
    x = x + (jnp.sum(gates * z, axis=2) @ lp['w_out']).astype(x.dtype)
    hf = _rms(x, lp['norm_ffn'])
    ff = _peer(hf.reshape(b * t, D_MODEL), lp['peer_wq'], lp['peer_subkeys'], lp['peer_u'], lp['peer_v'])
    x = x + ff.reshape(b, t, D_MODEL).astype(x.dtype)
    return x, moba_new, nsa_new, diff_new, dsa_new, _last_rows(wrows, win_buf)


def kernel(x_prompt, x_sample, cache_moba, cache_nsa, cache_diff, cache_dsa, state_nsa_win, page_table,
           rel_bias, norm_mix, w_in, qk_gain, nsa_pe, nsa_w1, nsa_w2, diff_qk_gain, diff_lambda,
           diff_out_gain, w_branch, w_out, norm_ffn, peer_wq, peer_subkeys, peer_u, peer_v):
    depth = w_in.shape[0]
    past_len = page_table.shape[1] * cache_moba.shape[2]
    win_buf = state_nsa_win.shape[2]
    qpos_p = jnp.arange(x_prompt.shape[1], dtype=jnp.int32)
    qpos_s = past_len + jnp.arange(x_sample.shape[1], dtype=jnp.int32)

    def paged(cache_l):
        g = cache_l[page_table]
        return g.reshape((g.shape[0], g.shape[1] * g.shape[2]) + g.shape[3:])

    yp, ys = x_prompt, x_sample
    rows_p, rows_s = [], []
    for l in range(depth):
        lp = {'norm_mix': norm_mix[l], 'w_in': w_in[l], 'qk_gain': qk_gain[l], 'nsa_pe': nsa_pe[l],
              'nsa_w1': nsa_w1[l], 'nsa_w2': nsa_w2[l], 'diff_qk_gain': diff_qk_gain[l],
              'diff_lambda': diff_lambda[l], 'diff_out_gain': diff_out_gain[l], 'w_branch': w_branch[l],
              'w_out': w_out[l], 'norm_ffn': norm_ffn[l], 'peer_wq': peer_wq[l],
              'peer_subkeys': peer_subkeys[l], 'peer_u': peer_u[l], 'peer_v': peer_v[l]}
        lam_init = 0.8 - 0.6 * math.exp(-0.3 * l)
        out_p = _layer(yp, qpos_p, None, None, 0, win_buf, lp, rel_bias, lam_init)
        past = (paged(cache_moba[l]), paged(cache_nsa[l]), paged(cache_diff[l]), paged(cache_dsa[l]))
        out_s = _layer(ys, qpos_s, past, state_nsa_win[l], past_len - win_buf, win_buf, lp, rel_bias, lam_init)
        yp, ys = out_p[0], out_s[0]
        rows_p.append(out_p[1:])
        rows_s.append(out_s[1:])

    def stk(rows, i):
        return jnp.stack([r[i] for r in rows], axis=0)

    return (yp, ys, stk(rows_p, 0), stk(rows_s, 0), stk(rows_p, 1), stk(rows_s, 1),
            stk(rows_p, 2), stk(rows_s, 2), stk(rows_p, 3), stk(rows_s, 3), stk(rows_p, 4), stk(rows_s, 4))
```

```python
import functools
import math

import jax
import jax.numpy as jnp
import numpy as np
from jax import lax
from jax.experimental import pallas as pl
from jax.experimental.pallas import tpu as pltpu

D_MODEL = 1024
N_MIXERS = 4
N_HEADS = 4
HEAD_DIM = 64
MIX_W = N_HEADS * HEAD_DIM
REL_BUCKETS = 32
REL_MAX_DIST = 128
MOBA_BLOCK = 256
MOBA_TOPK = 3
MOBA_QB = 32
NSA_CMP_LEN = 32
NSA_CMP_STRIDE = 16
NSA_CMP_HID = 128
NSA_SLC_BLOCK = 64
NSA_N_SEL = 16
NSA_WINDOW = 512
DIFF_DH = 32
DSA_IDX_HEADS = 4
DSA_IDX_DIM = 32
DSA_TOPK = 256
DSA_ROW = 2 * HEAD_DIM + DSA_IDX_DIM
PEER_HEADS = 8
PEER_KEYS = 128
PEER_TOPK = 16
PEER_QDIM = 256
PEER_CHUNK = 128
Q_BLOCK = 128
EPS = 1e-6
NEG = -1e30
BIG = 1e9

IN_SPLITS = (
    ('moba_q', MIX_W), ('moba_k', MIX_W), ('moba_v', MIX_W),
    ('nsa_q', MIX_W), ('nsa_kc', HEAD_DIM), ('nsa_vc', HEAD_DIM), ('nsa_ks', HEAD_DIM),
    ('nsa_vs', HEAD_DIM), ('nsa_kw', HEAD_DIM), ('nsa_vw', HEAD_DIM), ('nsa_g', 3 * N_HEADS),
    ('diff_q', N_HEADS * 2 * DIFF_DH), ('diff_k', N_HEADS * 2 * DIFF_DH), ('diff_v', N_HEADS * 2 * DIFF_DH),
    ('dsa_q', MIX_W), ('dsa_k', HEAD_DIM), ('dsa_v', HEAD_DIM),
    ('dsa_qi', DSA_IDX_HEADS * DSA_IDX_DIM), ('dsa_ki', DSA_IDX_DIM), ('dsa_w', DSA_IDX_HEADS),
    ('gates', N_MIXERS * D_MODEL),
)
IN_COLS = sum(s for _, s in IN_SPLITS)
IN_OFF = {}
_o = 0
for _n, _s in IN_SPLITS:
    IN_OFF[_n] = (_o, _s)
    _o += _s

LANE = 128
TILE = 256
VMEM_LIMIT = 56 * 1024 * 1024
F32 = jnp.float32
BF16 = jnp.bfloat16
INT_MIN = -2 ** 31

N64_W = 1280
N32_W = 512
RAW_W = 1152
MAIN_W = N64_W + N32_W + RAW_W
MISC_G = 0
MISC_W = 12


def _dot(a, b):
    return jnp.dot(a, b, preferred_element_type=F32)


def _dot_nt(a, b):
    return lax.dot_general(a, b, (((1,), (1,)), ((), ())), preferred_element_type=F32)


def _cparams(n_grid):
    return pltpu.CompilerParams(dimension_semantics=("arbitrary",) * n_grid, vmem_limit_bytes=VMEM_LIMIT)


def _t5_bucket(rel):
    n = jnp.maximum(rel, 0)
    exact = REL_BUCKETS // 2
    nf = jnp.maximum(n, 1).astype(jnp.float32)
    large = exact + (jnp.log(nf / exact) / math.log(REL_MAX_DIST / exact) * (REL_BUCKETS - exact)).astype(jnp.int32)
    return jnp.where(n < exact, n, jnp.minimum(large, REL_BUCKETS - 1))


def _bias_tiles(tab, tq, tk, n_d, window=None):
    r = jnp.arange(tq)[:, None]
    c = jnp.arange(tk)[None, :]
    rel = jnp.arange(n_d)[:, None, None] * tq + r - c
    ok = rel >= 0
    if window is not None:
        ok = ok & (rel < window)
    b = jnp.where(ok[..., None], tab[_t5_bucket(rel)], NEG)
    return jnp.transpose(b, (3, 0, 1, 2)).astype(F32)


def _prep_w_in(w_in, qk_gain, diff_qk_gain):
    def col(name):
        o, s = IN_OFF[name]
        return w_in[:, o:o + s]

    z = lambda n: jnp.zeros((D_MODEL, n), w_in.dtype)
    w_main = jnp.concatenate([
        col('moba_q'), col('moba_k'), col('nsa_q'), col('dsa_q'), col('nsa_ks'), col('nsa_kw'), col('dsa_k'), z(64),
        col('diff_q'), col('diff_k'),
        col('moba_v'), col('diff_v'), col('nsa_kc'), col('nsa_vc'), col('nsa_vs'), col('nsa_vw'), col('dsa_v'),
        col('dsa_ki'), z(32), col('dsa_qi'), col('nsa_g'), col('dsa_w'), z(LANE - 16)], axis=1).astype(BF16)
    t4 = lambda g: jnp.tile(g, N_HEADS)
    g64 = jnp.concatenate([t4(qk_gain[0]), t4(qk_gain[1]), t4(qk_gain[2]), t4(qk_gain[6]),
                           qk_gain[4], qk_gain[5], qk_gain[7], jnp.zeros((64,), F32)]).reshape(1, N64_W)
    g32 = jnp.concatenate([jnp.tile(diff_qk_gain[0], 8), jnp.tile(diff_qk_gain[1], 8)]).reshape(1, N32_W)
    w_gate = col('gates').astype(BF16)
    return w_main, g64, g32, w_gate


def _block_diag_ones(seg):
    i = np.arange(LANE)
    return jnp.asarray((i[:, None] // seg) == (i[None, :] // seg), BF16)


def _inproj_kernel(x_ref, g_ref, w_ref, g64_ref, g32_ref, bd64_ref, bd32_ref,
                   mq_ref, nq_ref, dq_ref, sq_ref, sqi_ref, misc_ref,
                   moba_ref, nsa_ref, win_ref, diff_ref, dsa_ref, kmean_ref):
    x = x_ref[...]
    hn = (x * lax.rsqrt(jnp.mean(x * x, axis=-1, keepdims=True) + EPS) * g_ref[...]).astype(BF16)

    def segnorm(p, bd, seg, gain):
        x2 = p * p
        hi = x2.astype(BF16)
        lo = (x2 - hi.astype(F32)).astype(BF16)
        ss = _dot(hi, bd) + _dot(lo, bd)
        return p * lax.rsqrt(ss * (1.0 / seg) + EPS) * gain

    p1 = _dot(hn, w_ref[:, 0:N64_W])
    bd64 = bd64_ref[...]
    c = [segnorm(p1[:, i * LANE:(i + 1) * LANE], bd64, 64, g64_ref[:, i * LANE:(i + 1) * LANE])
         for i in range(N64_W // LANE)]
    mq_ref[:, 0:128] = c[0]
    mq_ref[:, 128:256] = c[1]
    moba_ref[:, 0:128] = c[2]
    moba_ref[:, 128:256] = c[3]
    kmean_ref[0] = jnp.concatenate([jnp.mean(c[2], axis=0, keepdims=True),
                                    jnp.mean(c[3], axis=0, keepdims=True)], axis=1)
    nq_ref[:, 0:128] = c[4]
    nq_ref[:, 128:256] = c[5]
    sq_ref[:, 0:128] = c[6]
    sq_ref[:, 128:256] = c[7]
    nsa_ref[:, 128:192] = c[8][:, 0:64]
    win_ref[:, 0:64] = c[8][:, 64:128]
    dsa_ref[:, 0:64] = c[9][:, 0:64]

    p2 = _dot(hn, w_ref[:, N64_W:N64_W + N32_W])
    bd32 = bd32_ref[...]
    d = [segnorm(p2[:, i * LANE:(i + 1) * LANE], bd32, 32, g32_ref[:, i * LANE:(i + 1) * LANE])
         for i in range(N32_W // LANE)]
    dq_ref[:, 0:128] = d[0]
    dq_ref[:, 128:256] = d[1]
    diff_ref[:, 0:128] = d[2]
    diff_ref[:, 128:256] = d[3]

    p3 = _dot(hn, w_ref[:, N64_W + N32_W:MAIN_W])
    moba_ref[:, 256:512] = p3[:, 0:256]
    diff_ref[:, 256:512] = p3[:, 256:512]
    nsa_ref[:, 0:128] = p3[:, 512:640]
    nsa_ref[:, 192:256] = p3[:, 640:704]
    win_ref[:, 64:128] = p3[:, 704:768]
    dsa_ref[:, 64:128] = p3[:, 768:832]
    dsa_ref[:, 128:160] = p3[:, 832:864]
    sqi_ref[...] = p3[:, 896:1024]
    misc_ref[...] = p3[:, 1024:1152]


def _in_proj(x2d, norm_g, w_main, g64, g32, tm):
    n = x2d.shape[0]
    row = lambda w: pl.BlockSpec((tm, w), lambda i: (i, 0))
    full = lambda a: pl.BlockSpec(a.shape, lambda i: (0,) * a.ndim)
    bd64, bd32 = _block_diag_ones(64), _block_diag_ones(32)
    g = norm_g.reshape(1, D_MODEL)
    widths = (MIX_W, MIX_W, MIX_W, MIX_W, 128, 128, 512, 256, 128, 512, DSA_ROW)
    out_shape = [jax.ShapeDtypeStruct((n, w), F32) for w in widths]
    out_shape.append(jax.ShapeDtypeStruct((n // tm, 1, MIX_W), F32))
    out_specs = [row(w) for w in widths] + [pl.BlockSpec((1, 1, MIX_W), lambda i: (i, 0, 0))]
    return pl.pallas_call(
        _inproj_kernel,
        grid=(n // tm,),
        in_specs=[row(D_MODEL), full(g), full(w_main), full(g64), full(g32), full(bd64), full(bd32)],
        out_specs=out_specs,
        out_shape=out_shape,
        compiler_params=_cparams(1),
    )(x2d, g, w_main, g64, g32, bd64, bd32)


def _flash_init(m_sc, l_sc, acc_sc):
    m_sc[...] = jnp.full(m_sc.shape, NEG, F32)
    l_sc[...] = jnp.zeros(l_sc.shape, F32)
    acc_sc[...] = jnp.zeros(acc_sc.shape, F32)


def _flash_update(s, i, v_bf, m_sc, l_sc, acc_sc):
    m_old = m_sc[i]
    m_new = jnp.maximum(m_old, jnp.max(s, axis=-1, keepdims=True))
    alpha = jnp.exp(m_old - m_new)
    p = jnp.exp(s - m_new)
    l_sc[i] = alpha * l_sc[i] + jnp.sum(p, axis=-1, keepdims=True)
    acc_sc[i] = alpha * acc_sc[i] + _dot(p.astype(BF16), v_bf)
    m_sc[i] = m_new


def _flash_scratch(n_streams, tq, dv):
    return [pltpu.VMEM((n_streams, tq, 1), F32), pltpu.VMEM((n_streams, tq, 1), F32),
            pltpu.VMEM((n_streams, tq, dv), F32)]


def _diff_kernel(q_ref, k_ref, v_ref, bt_ref, dl_ref, gain_ref, o_ref, m_sc, l_sc, acc_sc, *, lam_init):
    qi = pl.program_id(1)
    tq = q_ref.shape[1]
    scale = DIFF_DH ** -0.5
    q = q_ref[0].astype(BF16)
    _flash_init(m_sc, l_sc, acc_sc)

    def body(kj, carry):
        d = jnp.minimum(qi - kj, 2)
        off = pl.multiple_of(kj * tq, tq)
        k = k_ref[0, pl.ds(off, tq), :]
        v = v_ref[0, pl.ds(off, tq), :]
        for h in range(N_HEADS):
            bias = bt_ref[h, d]
            vh = v[:, h * HEAD_DIM:(h + 1) * HEAD_DIM]
            for m in range(2):
                lo = h * HEAD_DIM + m * DIFF_DH
                s = _dot_nt(q[:, lo:lo + DIFF_DH], k[:, lo:lo + DIFF_DH]) * scale + bias
                _flash_update(s, 2 * h + m, vh, m_sc, l_sc, acc_sc)
        return carry

    lax.fori_loop(0, qi + 1, body, 0)
    dl = dl_ref[...]
    lam = (jnp.exp(jnp.sum(dl[0:1] * dl[1:2], axis=-1, keepdims=True))
           - jnp.exp(jnp.sum(dl[2:3] * dl[3:4], axis=-1, keepdims=True)) + lam_init)
    for h in range(N_HEADS):
        o = acc_sc[2 * h] / l_sc[2 * h] - lam * (acc_sc[2 * h + 1] / l_sc[2 * h + 1])
        o = o * lax.rsqrt(jnp.mean(o * o, axis=-1, keepdims=True) + EPS) * gain_ref[...] * (1.0 - lam_init)
        o_ref[0, :, h * HEAD_DIM:(h + 1) * HEAD_DIM] = o


def _diff_prompt(dq, dk_bf, dv_bf, tab, dl, out_gain, lam_init):
    b, t, _ = dq.shape
    tq = TILE
    bt = _bias_tiles(tab, tq, tq, 3)
    gain = out_gain.reshape(1, HEAD_DIM)
    full = lambda a: pl.BlockSpec(a.shape, lambda i, j: (0,) * a.ndim)
    return pl.pallas_call(
        functools.partial(_diff_kernel, lam_init=lam_init),
        grid=(b, t // tq),
        in_specs=[pl.BlockSpec((1, tq, MIX_W), lambda i, j: (i, j, 0)),
                  pl.BlockSpec((1, t, MIX_W), lambda i, j: (i, 0, 0)),
                  pl.BlockSpec((1, t, MIX_W), lambda i, j: (i, 0, 0)),
                  full(bt), full(dl), full(gain)],
        out_specs=pl.BlockSpec((1, tq, MIX_W), lambda i, j: (i, j, 0)),
        out_shape=jax.ShapeDtypeStruct((b, t, MIX_W), F32),
        scratch_shapes=_flash_scratch(2 * N_HEADS, tq, HEAD_DIM),
        compiler_params=_cparams(2),
    )(dq, dk_bf, dv_bf, bt, dl, gain)


def _moba_kernel(q_ref, k_ref, v_ref, km_ref, bt_ref, o_ref, sel_sc, m_sc, l_sc, acc_sc):
    qi = pl.program_id(1)
    tq = q_ref.shape[1]
    scale = HEAD_DIM ** -0.5
    q = q_ref[0].astype(BF16)
    km = km_ref[0].astype(BF16)
    jidx = lax.broadcasted_iota(jnp.int32, (tq, LANE), 1)
    _flash_init(m_sc, l_sc, acc_sc)
    for h in range(N_HEADS):
        hs = slice(h * HEAD_DIM, (h + 1) * HEAD_DIM)
        g = _dot_nt(q[:, hs], km[:, hs])
        g = jnp.where(jidx < qi, g, -BIG)
        sel = jidx == qi
        for _ in range(MOBA_TOPK):
            mx = jnp.max(g, axis=-1, keepdims=True)
            first = jnp.min(jnp.where(g == mx, jidx, LANE), axis=-1, keepdims=True)
            hit = jidx == first
            sel = sel | (hit & (mx > -0.5 * BIG))
            g = jnp.where(hit, -3.0 * BIG, g)
        sel_sc[h] = sel.astype(F32)

    def body(kj, carry):
        d = jnp.minimum(qi - kj, 2)
        off = pl.multiple_of(kj * tq, tq)
        k = k_ref[0, pl.ds(off, tq), :]
        v = v_ref[0, pl.ds(off, tq), :]
        for h in range(N_HEADS):
            hs = slice(h * HEAD_DIM, (h + 1) * HEAD_DIM)
            rowsel = jnp.max(jnp.where(jidx == kj, sel_sc[h], 0.0), axis=-1, keepdims=True) > 0.5
            s = _dot_nt(q[:, hs], k[:, hs]) * scale + bt_ref[h, d]
            s = jnp.where(rowsel, s, NEG)
            _flash_update(s, h, v[:, hs], m_sc, l_sc, acc_sc)
        return carry

    lax.fori_loop(0, qi + 1, body, 0)
    for h in range(N_HEADS):
        o_ref[0, :, h * HEAD_DIM:(h + 1) * HEAD_DIM] = acc_sc[h] / l_sc[h]


def _moba_prompt(mq, k_bf, v_bf, kmean, tab):
    b, t, _ = mq.shape
    tq = TILE
    assert tq == MOBA_BLOCK and t % tq == 0 and t // tq <= LANE
    bt = _bias_tiles(tab, tq, tq, 3)
    km = jnp.pad(kmean, ((0, 0), (0, LANE - kmean.shape[1]), (0, 0)))
    full = lambda a: pl.BlockSpec(a.shape, lambda i, j: (0,) * a.ndim)
    return pl.pallas_call(
        _moba_kernel,
        grid=(b, t // tq),
        in_specs=[pl.BlockSpec((1, tq, MIX_W), lambda i, j: (i, j, 0)),
                  pl.BlockSpec((1, t, MIX_W), lambda i, j: (i, 0, 0)),
                  pl.BlockSpec((1, t, MIX_W), lambda i, j: (i, 0, 0)),
                  pl.BlockSpec((1, LANE, MIX_W), lambda i, j: (i, 0, 0)),
                  full(bt)],
        out_specs=pl.BlockSpec((1, tq, MIX_W), lambda i, j: (i, j, 0)),
        out_shape=jax.ShapeDtypeStruct((b, t, MIX_W), F32),
        scratch_shapes=[pltpu.VMEM((N_HEADS, tq, LANE), F32)] + _flash_scratch(N_HEADS, tq, HEAD_DIM),
        compiler_params=_cparams(2),
    )(mq, k_bf, v_bf, km, bt)


def _sortable(x):
    x = jnp.where(x == 0.0, 0.0, x)
    bits = pltpu.bitcast(x, jnp.int32)
    return bits ^ ((bits >> 31) & 0x7FFFFFFF)


def _dsa_kernel(q_ref, qi_ref, misc_ref, k_ref, v_ref, ki_ref, bt_ref, tri_ref, o_ref,
                key_sc, m_sc, l_sc, acc_sc):
    qi = pl.program_id(1)
    tq = q_ref.shape[1]
    scale = HEAD_DIM ** -0.5
    q = q_ref[0].astype(BF16)
    qidx = qi_ref[0].astype(BF16)
    w = misc_ref[0][:, MISC_W:MISC_W + DSA_IDX_HEADS]
    r_io = lax.broadcasted_iota(jnp.int32, (tq, tq), 0)
    c_io = lax.broadcasted_iota(jnp.int32, (tq, tq), 1)
    nk = qi + 1

    def score_body(kj, carry):
        off = pl.multiple_of(kj * tq, tq)
        ki = ki_ref[0, pl.ds(off, tq), :]
        sc = jnp.zeros((tq, tq), F32)
        for h in range(DSA_IDX_HEADS):
            idx = _dot_nt(qidx[:, h * DSA_IDX_DIM:(h + 1) * DSA_IDX_DIM], ki)
            sc = sc + jnp.maximum(idx, 0.0) * w[:, h:h + 1]
        sc = jnp.where((kj < qi) | (c_io <= r_io), sc, -BIG)
        key_sc[kj] = _sortable(sc)
        return carry

    lax.fori_loop(0, nk, score_body, 0)

    def count_ge(tau):
        def cb(kj, cnt):
            return cnt + jnp.sum(jnp.where(key_sc[kj] >= tau, 1.0, 0.0), axis=-1, keepdims=True)
        return lax.fori_loop(0, nk, cb, jnp.zeros((tq, 1), F32))

    def bis_body(it, tau):
        cand = tau + lax.shift_left(jnp.int32(1), 31 - it)
        return jnp.where(count_ge(cand) >= float(DSA_TOPK), cand, tau)

    tau = lax.fori_loop(0, 32, bis_body, jnp.full((tq, 1), INT_MIN, jnp.int32))
    n_gt = count_ge(tau + 1)
    need = float(DSA_TOPK) - n_gt
    tri = tri_ref[...]
    _flash_init(m_sc, l_sc, acc_sc)

    def att_body(kj, run):
        d = jnp.minimum(qi - kj, 2)
        off = pl.multiple_of(kj * tq, tq)
        key = key_sc[kj]
        eq = key == tau
        pref = run + _dot(jnp.where(eq, 1.0, 0.0).astype(BF16), tri)
        mask = (key > tau) | (eq & (pref <= need))
        k = k_ref[0, pl.ds(off, tq), :]
        v = v_ref[0, pl.ds(off, tq), :]
        for h in range(N_HEADS):
            s = _dot_nt(q[:, h * HEAD_DIM:(h + 1) * HEAD_DIM], k) * scale + bt_ref[h, d]
            s = jnp.where(mask, s, NEG)
            _flash_update(s, h, v, m_sc, l_sc, acc_sc)
        return run + jnp.sum(jnp.where(eq, 1.0, 0.0), axis=-1, keepdims=True)

    lax.fori_loop(0, nk, att_body, jnp.zeros((tq, 1), F32))
    for h in range(N_HEADS):
        o_ref[0, :, h * HEAD_DIM:(h + 1) * HEAD_DIM] = acc_sc[h] / l_sc[h]


def _dsa_prompt(sq, sqi, misc, k_bf, v_bf, ki_bf, tab):
    b, t, _ = sq.shape
    tq = TILE
    assert tq >= DSA_TOPK and t // 4 >= DSA_TOPK
    bt = _bias_tiles(tab, tq, tq, 3)
    i = np.arange(tq)
    tri = jnp.asarray(i[:, None] <= i[None, :], BF16)
    full = lambda a: pl.BlockSpec(a.shape, lambda i, j: (0,) * a.ndim)
    seq = lambda w: pl.BlockSpec((1, t, w), lambda i, j: (i, 0, 0))
    til = lambda w: pl.BlockSpec((1, tq, w), lambda i, j: (i, j, 0))
    return pl.pallas_call(
        _dsa_kernel,
        grid=(b, t // tq),
        in_specs=[til(MIX_W), til(128), til(128), seq(HEAD_DIM), seq(HEAD_DIM), seq(DSA_IDX_DIM), full(bt), full(tri)],
        out_specs=til(MIX_W),
        out_shape=jax.ShapeDtypeStruct((b, t, MIX_W), F32),
        scratch_shapes=[pltpu.VMEM((t // tq, tq, tq), jnp.int32)] + _flash_scratch(N_HEADS, tq, HEAD_DIM),
        compiler_params=_cparams(2),
    )(sq, sqi, misc, k_bf, v_bf, ki_bf, bt, tri)


def _compress_kernel(c_ref, pe_ref, w1_ref, w2_ref, g_ref, o_ref):
    half = NSA_CMP_STRIDE * HEAD_DIM
    for i in range(2):
        c = c_ref[0, i]
        a = _dot((c + pe_ref[i, :, 0:half]).astype(BF16), w1_ref[i, 0:half, :])
        bm = _dot((c + pe_ref[i, :, half:2 * half]).astype(BF16), w1_ref[i, half:2 * half, :])
        hid = a + jnp.concatenate([bm[1:], jnp.zeros((1, bm.shape[1]), F32)], axis=0)
        y = _dot(jax.nn.gelu(hid).astype(BF16), w2_ref[i])
        if i == 0:
            y = y * lax.rsqrt(jnp.mean(y * y, axis=-1, keepdims=True) + EPS) * g_ref[...]
        o_ref[0, i] = y


def _nsa_compress(kc, vc, pe, w1, w2, kn_gain):
    b, tc, _ = kc.shape
    nch = tc // NSA_CMP_STRIDE
    c = jnp.stack([kc, vc], axis=1).reshape(b, 2, nch, NSA_CMP_STRIDE * HEAD_DIM)
    pe2 = pe.reshape(2, 1, NSA_CMP_LEN * HEAD_DIM)
    full = lambda a: pl.BlockSpec(a.shape, lambda i: (0,) * a.ndim)
    g = kn_gain.reshape(1, HEAD_DIM)
    w1b, w2b = w1.astype(BF16), w2.astype(BF16)
    return pl.pallas_call(
        _compress_kernel,
        grid=(b,),
        in_specs=[pl.BlockSpec((1, 2, nch, NSA_CMP_STRIDE * HEAD_DIM), lambda i: (i, 0, 0, 0)),
                  full(pe2), full(w1b), full(w2b), full(g)],
        out_specs=pl.BlockSpec((1, 2, nch, HEAD_DIM), lambda i: (i, 0, 0, 0)),
        out_shape=jax.ShapeDtypeStruct((b, 2, nch, HEAD_DIM), F32),
        compiler_params=_cparams(1),
    )(c, pe2, w1b, w2b, g)


def _nsa_kernel(q_ref, misc_ref, cmp_ref, ks_ref, vs_ref, kw_ref, vw_ref, cb_ref, ov_ref, ex_ref,
                bt_ref, wbt_ref, o_ref, m_sc, l_sc, acc_sc, oc_sc):
    qi = pl.program_id(1)
    tq = q_ref.shape[1]
    n_slc = ov_ref.shape[1]
    scale = HEAD_DIM ** -0.5
    q = q_ref[0].astype(BF16)
    kcmp = cmp_ref[0, 0].astype(BF16)
    vcmp = cmp_ref[0, 1].astype(BF16)
    ov = ov_ref[...]

    imp = jnp.zeros((tq, n_slc), F32)
    for h in range(N_HEADS):
        cb = cb_ref[h]
        ok = cb > 0.5 * NEG
        z = jnp.where(ok, _dot_nt(q[:, h * HEAD_DIM:(h + 1) * HEAD_DIM], kcmp) * scale + cb, NEG)
        z = z - jnp.max(z, axis=-1, keepdims=True)
        p = jnp.where(ok, jnp.exp(z), 0.0)
        p = p / jnp.maximum(jnp.sum(p, axis=-1, keepdims=True), 1e-30)
        pb = p.astype(BF16)
        oc_sc[h] = _dot(pb, vcmp)
        imp = imp + _dot(pb, ov)

    jidx = lax.broadcasted_iota(jnp.int32, (tq, n_slc), 1)
    pos = qi * tq + lax.broadcasted_iota(jnp.int32, (tq, n_slc), 0)
    bt = pos // NSA_SLC_BLOCK
    forced = (jidx == 0) | (jidx == bt) | (jidx == bt - 1)
    score = jnp.where(jidx <= bt, jnp.where(forced, BIG, imp), -BIG)
    sel = jnp.zeros((tq, n_slc), F32)
    for _ in range(min(NSA_N_SEL, n_slc)):
        mx = jnp.max(score, axis=-1, keepdims=True)
        first = jnp.min(jnp.where(score == mx, jidx, n_slc), axis=-1, keepdims=True)
        hit = jidx == first
        sel = jnp.where(hit, 1.0, sel)
        score = jnp.where(hit, -3.0 * BIG, score)
    sel_bf = sel.astype(BF16)

    _flash_init(m_sc, l_sc, acc_sc)

    def slc_body(kj, carry):
        d = jnp.minimum(qi - kj, 2)
        off = pl.multiple_of(kj * tq, tq)
        mask = _dot(sel_bf, ex_ref[kj]) > 0.5
        k = ks_ref[0, pl.ds(off, tq), :]
        v = vs_ref[0, pl.ds(off, tq), :]
        for h in range(N_HEADS):
            s = _dot_nt(q[:, h * HEAD_DIM:(h + 1) * HEAD_DIM], k) * scale + bt_ref[h, d]
            s = jnp.where(mask, s, NEG)
            _flash_update(s, h, v, m_sc, l_sc, acc_sc)
        return carry

    lax.fori_loop(0, qi + 1, slc_body, 0)

    n_w = wbt_ref.shape[1]

    def win_body(kj, carry):
        d = qi - kj
        off = pl.multiple_of(kj * tq, tq)
        k = kw_ref[0, pl.ds(off, tq), :]
        v = vw_ref[0, pl.ds(off, tq), :]
        for h in range(N_HEADS):
            s = _dot_nt(q[:, h * HEAD_DIM:(h + 1) * HEAD_DIM], k) * scale + wbt_ref[h, d]
            _flash_update(s, N_HEADS + h, v, m_sc, l_sc, acc_sc)
        return carry

    lax.fori_loop(jnp.maximum(qi - (n_w - 1), 0), qi + 1, win_body, 0)

    gl = misc_ref[0][:, MISC_G:MISC_G + 3 * N_HEADS]
    g = 1.0 / (1.0 + jnp.exp(-gl))
    for h in range(N_HEADS):
        o = (g[:, 3 * h:3 * h + 1] * oc_sc[h]
             + g[:, 3 * h + 1:3 * h + 2] * (acc_sc[h] / l_sc[h])
             + g[:, 3 * h + 2:3 * h + 3] * (acc_sc[N_HEADS + h] / l_sc[N_HEADS + h]))
        o_ref[0, :, h * HEAD_DIM:(h + 1) * HEAD_DIM] = o


def _nsa_prompt(nq, misc, cmp_kv, ks_bf, vs_bf, kw_bf, vw_bf, tab):
    b, t, _ = nq.shape
    tq = TILE
    n_cmp = (t - NSA_CMP_LEN) // NSA_CMP_STRIDE + 1
    n_cpad = cmp_kv.shape[2]
    n_slc = t // NSA_SLC_BLOCK
    assert t % tq == 0 and tq % NSA_SLC_BLOCK == 0 and n_cpad >= n_cmp
    cidx = jnp.arange(n_cpad)
    crel = jnp.arange(t)[:, None] - (cidx * NSA_CMP_STRIDE + NSA_CMP_LEN - 1)[None, :]
    cok = (crel >= 0) & (cidx < n_cmp)[None, :]
    cbias = jnp.transpose(jnp.where(cok[..., None], tab[_t5_bucket(crel)], NEG), (2, 0, 1)).astype(F32)
    cstart = cidx * NSA_CMP_STRIDE
    sstart = jnp.arange(n_slc) * NSA_SLC_BLOCK
    overlap = ((cstart[:, None] < sstart[None, :] + NSA_SLC_BLOCK)
               & (cstart[:, None] + NSA_CMP_LEN > sstart[None, :]) & (cidx < n_cmp)[:, None]).astype(BF16)
    kpos = jnp.arange(t).reshape(t // tq, 1, tq)
    expand = (kpos // NSA_SLC_BLOCK == jnp.arange(n_slc)[None, :, None]).astype(BF16)
    bt = _bias_tiles(tab, tq, tq, 3)
    n_w = (NSA_WINDOW - 1 + tq - 1) // tq + 1
    wbt = _bias_tiles(tab, tq, tq, n_w, window=NSA_WINDOW)
    full = lambda a: pl.BlockSpec(a.shape, lambda i, j: (0,) * a.ndim)
    seq = lambda w: pl.BlockSpec((1, t, w), lambda i, j: (i, 0, 0))
    til = lambda w: pl.BlockSpec((1, tq, w), lambda i, j: (i, j, 0))
    return pl.pallas_call(
        _nsa_kernel,
        grid=(b, t // tq),
        in_specs=[til(MIX_W), til(128),
                  pl.BlockSpec((1, 2, n_cpad, HEAD_DIM), lambda i, j: (i, 0, 0, 0)),
                  seq(HEAD_DIM), seq(HEAD_DIM), seq(HEAD_DIM), seq(HEAD_DIM),
                  pl.BlockSpec((N_HEADS, tq, n_cpad), lambda i, j: (0, j, 0)),
                  full(overlap), full(expand), full(bt), full(wbt)],
        out_specs=til(MIX_W),
        out_shape=jax.ShapeDtypeStruct((b, t, MIX_W), F32),
        scratch_shapes=_flash_scratch(2 * N_HEADS, tq, HEAD_DIM) + [pltpu.VMEM((N_HEADS, tq, HEAD_DIM), F32)],
        compiler_params=_cparams(2),
    )(nq, misc, cmp_kv, ks_bf, vs_bf, kw_bf, vw_bf, cbias, overlap, expand, bt, wbt)


def _merge_kernel(x_ref, g_ref, wg_ref, oa_ref, ob_ref, oc_ref, od_ref, wb_ref, wo_ref, gf_ref, wq_ref,
                  x1_ref, hf_ref, qh_ref):
    x = x_ref[...]
    hn = (x * lax.rsqrt(jnp.mean(x * x, axis=-1, keepdims=True) + EPS) * g_ref[...]).astype(BF16)
    merged = jnp.zeros(x.shape, F32)
    for i, o_ref in enumerate((oa_ref, ob_ref, oc_ref, od_ref)):
        gl = _dot(hn, wg_ref[:, i * D_MODEL:(i + 1) * D_MODEL])
        z = _dot(o_ref[...].astype(BF16), wb_ref[i])
        merged = merged + z / (1.0 + jnp.exp(-gl))
    x1 = x + _dot(merged.astype(BF16), wo_ref[...])
    x1_ref[...] = x1
    hf = (x1 * lax.rsqrt(jnp.mean(x1 * x1, axis=-1, keepdims=True) + EPS) * gf_ref[...])
    hfb = hf.astype(BF16)
    hf_ref[...] = hfb
    qh_ref[...] = _dot(hfb, wq_ref[...])


def _merge(x2d, norm_g, w_gate, o_a, o_b, o_c, o_d, w_branch, w_out, norm_ffn, wq, tm):
    n = x2d.shape[0]
    row = lambda w: pl.BlockSpec((tm, w), lambda i: (i, 0))
    full = lambda a: pl.BlockSpec(a.shape, lambda i: (0,) * a.ndim, pipeline_mode=pl.Buffered(1))
    g = norm_g.reshape(1, D_MODEL)
    gf = norm_ffn.reshape(1, D_MODEL)
    wb = w_branch.astype(BF16)
    wo = w_out.astype(BF16)
    wqb = wq.astype(BF16)
    nq = wq.shape[1]
    return pl.pallas_call(
        _merge_kernel,
        grid=(n // tm,),
        in_specs=[row(D_MODEL), full(g), full(w_gate), row(MIX_W), row(MIX_W), row(MIX_W), row(MIX_W),
                  full(wb), full(wo), full(gf), full(wqb)],
        out_specs=[row(D_MODEL), row(D_MODEL), row(nq)],
        out_shape=[jax.ShapeDtypeStruct((n, D_MODEL), F32), jax.ShapeDtypeStruct((n, D_MODEL), BF16),
                   jax.ShapeDtypeStruct((n, nq), F32)],
        compiler_params=_cparams(1),
    )(x2d, g, w_gate, o_a, o_b, o_c, o_d, wb, wo, gf, wqb)


PEER_I1_CHUNK = 8
PEER_ECHUNK = PEER_I1_CHUNK * PEER_KEYS


def _top_desc(x, n, iota0):
    rows = []
    big = x.shape[0]
    for _ in range(n):
        mx = jnp.max(x, axis=0, keepdims=True)
        first = jnp.min(jnp.where(x == mx, iota0, big), axis=0, keepdims=True)
        x = jnp.where(iota0 == first, -jnp.inf, x)
        rows.append(mx)
    return rows


def _peer_kernel(hf_ref, qh_ref, sk_ref, u_ref, vt_ref, x1_ref, o_ref, s1_sc, s2_sc, e1_sc, e2_sc, tau_sc, acc_sc):
    c = pl.program_id(1)
    tm = hf_ref.shape[0]
    half = PEER_QDIM // 2

    @pl.when(c == 0)
    def _():
        acc_sc[...] = jnp.zeros(acc_sc.shape, F32)
        io_k = lax.broadcasted_iota(jnp.int32, (PEER_KEYS, tm), 0)
        io_c = lax.broadcasted_iota(jnp.int32, (PEER_TOPK * PEER_TOPK, tm), 0)
        for h in range(PEER_HEADS):
            qh = qh_ref[:, h * PEER_QDIM:(h + 1) * PEER_QDIM].astype(BF16)
            s1 = _dot_nt(sk_ref[0, h], qh[:, 0:half])
            s2 = _dot_nt(sk_ref[1, h], qh[:, half:PEER_QDIM])
            t1 = _top_desc(s1, PEER_TOPK, io_k)
            t2 = jnp.concatenate(_top_desc(s2, PEER_TOPK, io_k), axis=0)
            cand = jnp.concatenate([t1[a] + t2 for a in range(PEER_TOPK)], axis=0)
            ts = _top_desc(cand, PEER_TOPK, io_c)
            zsum = ts[0] * 0.0
            for a in range(PEER_TOPK):
                zsum = zsum + jnp.exp(ts[a] - ts[0])
            m1 = t1[0]
            m2 = t2[0:1]
            s1_sc[h] = s1
            s2_sc[h] = s2
            e1_sc[h] = jnp.exp(s1 - m1) / zsum
            e2_sc[h] = jnp.exp(s2 - m2)
            tau_sc[h] = jnp.broadcast_to(ts[PEER_TOPK - 1], (8, tm))

    act = jax.nn.gelu(_dot_nt(u_ref[...], hf_ref[...]))
    for j in range(PEER_I1_CHUNK):
        i1 = c * PEER_I1_CHUNK + j
        wj = jnp.zeros((PEER_KEYS, tm), F32)
        for h in range(PEER_HEADS):
            tot = s1_sc[h, pl.ds(i1, 1), :] + s2_sc[h]
            wj = wj + jnp.where(tot >= tau_sc[h, 0:1, :], e1_sc[h, pl.ds(i1, 1), :] * e2_sc[h], 0.0)
        gj = (wj * act[j * PEER_KEYS:(j + 1) * PEER_KEYS, :]).astype(BF16)
        acc_sc[...] += _dot(vt_ref[:, j * PEER_KEYS:(j + 1) * PEER_KEYS], gj)

    @pl.when(c == pl.num_programs(1) - 1)
    def _():
        o_ref[...] = x1_ref[...] + acc_sc[...].T


def _peer(hf_bf, qh, x1, subkeys, u_bf, vt_bf, tm):
    n = hf_bf.shape[0]
    n_exp = u_bf.shape[0]
    sk = subkeys.astype(BF16)
    nc = n_exp // PEER_ECHUNK
    return pl.pallas_call(
        _peer_kernel,
        grid=(n // tm, nc),
        in_specs=[pl.BlockSpec((tm, D_MODEL), lambda i, c: (i, 0)),
                  pl.BlockSpec((tm, PEER_HEADS * PEER_QDIM), lambda i, c: (i, 0)),
                  pl.BlockSpec(sk.shape, lambda i, c: (0, 0, 0, 0)),
                  pl.BlockSpec((PEER_ECHUNK, D_MODEL), lambda i, c: (c, 0)),
                  pl.BlockSpec((D_MODEL, PEER_ECHUNK), lambda i, c: (0, c)),
                  pl.BlockSpec((tm, D_MODEL), lambda i, c: (i, 0))],
        out_specs=pl.BlockSpec((tm, D_MODEL), lambda i, c: (i, 0)),
        out_shape=jax.ShapeDtypeStruct((n, D_MODEL), F32),
        scratch_shapes=[pltpu.VMEM((PEER_HEADS, PEER_KEYS, tm), F32)] * 4
        + [pltpu.VMEM((PEER_HEADS, 8, tm), F32), pltpu.VMEM((D_MODEL, tm), F32)],
        compiler_params=_cparams(2),
    )(hf_bf, qh, sk, u_bf, vt_bf, x1)


def _layer_prompt(x, lp, prep, rel_bias, lam_init):
    b, t, _ = x.shape
    n = b * t
    w_main, g64, g32, w_gate, u_bf, vt_bf = prep
    (mq, nq, dq, sq, sqi, misc, moba_new, nsa_new, win_new, diff_new, dsa_new, kmean) = _in_proj(
        x.reshape(n, D_MODEL), lp['norm_mix'], w_main, g64, g32, TILE)
    r3 = lambda a: a.reshape(b, t, a.shape[-1])
    moba3, nsa3, win3, diff3, dsa3 = r3(moba_new), r3(nsa_new), r3(win_new), r3(diff_new), r3(dsa_new)
    bf = lambda a: a.astype(BF16)
    o_a = _moba_prompt(r3(mq), bf(moba3[..., :MIX_W]), bf(moba3[..., MIX_W:]),
                       kmean.reshape(b, t // TILE, MIX_W), rel_bias[:, 0:4])
    t16 = t // NSA_CMP_STRIDE * NSA_CMP_STRIDE
    cmp_kv = _nsa_compress(nsa3[:, :t16, 0:64], nsa3[:, :t16, 64:128], lp['nsa_pe'], lp['nsa_w1'], lp['nsa_w2'],
                           lp['qk_gain'][3])
    o_b = _nsa_prompt(r3(nq), r3(misc), cmp_kv, bf(nsa3[..., 128:192]), bf(nsa3[..., 192:256]),
                      bf(win3[..., 0:64]), bf(win3[..., 64:128]), rel_bias[:, 4:8])
    o_c = _diff_prompt(r3(dq), bf(diff3[..., :MIX_W]), bf(diff3[..., MIX_W:]), rel_bias[:, 8:12],
                       lp['diff_lambda'].astype(F32), lp['diff_out_gain'], lam_init)
    o_d = _dsa_prompt(r3(sq), r3(sqi), r3(misc), bf(dsa3[..., 0:64]), bf(dsa3[..., 64:128]), bf(dsa3[..., 128:160]),
                      rel_bias[:, 12:16])
    f2 = lambda a: a.reshape(n, MIX_W)
    x1, hf_bf, qh = _merge(x.reshape(n, D_MODEL), lp['norm_mix'], w_gate, f2(o_a), f2(o_b), f2(o_c), f2(o_d),
                           lp['w_branch'], lp['w_out'], lp['norm_ffn'], lp['peer_wq'], TILE)
    x2 = _peer(hf_bf, qh, x1, lp['peer_subkeys'], u_bf, vt_bf, 512)
    return (x2.reshape(b, t, D_MODEL), moba3.reshape(b, t, 2, N_HEADS, HEAD_DIM), nsa3.reshape(b, t, 4, HEAD_DIM),
            diff3.reshape(b, t, 2, N_HEADS, HEAD_DIM), dsa3, win3.reshape(b, t, 2, HEAD_DIM))


def _masked_softmax(logits, mask):
    z = jnp.where(mask, logits.astype(F32), NEG)
    z = z - jnp.max(z, axis=-1, keepdims=True)
    p = jnp.where(mask, jnp.exp(z), 0.0)
    return p / jnp.maximum(jnp.sum(p, axis=-1, keepdims=True), 1e-30)


def _rms(x, g):
    return x * lax.rsqrt(jnp.mean(x * x, axis=-1, keepdims=True) + EPS) * g


def _moba_decode(q, pos, k, v, tab):
    b, tk = k.shape[0], k.shape[1]
    nblk = -(-tk // MOBA_BLOCK)
    pad = ((0, 0), (0, nblk * MOBA_BLOCK - tk), (0, 0), (0, 0))
    kb = jnp.pad(k, pad).reshape(b, nblk, MOBA_BLOCK, N_HEADS, HEAD_DIM)
    kmean = jnp.mean(kb, axis=2)
    bt = pos // MOBA_BLOCK
    jj = jnp.arange(nblk)
    gate = jnp.einsum('bhd,bjhd->bhj', q, kmean)
    gate = jnp.where(jj[None, None, :] < bt, gate, -BIG)
    ksel = min(MOBA_TOPK, nblk)
    _, top = lax.top_k(gate, ksel)
    blk_ok = jnp.any((top[..., None] == jj) & (top[..., None] < bt), axis=-2) | (jj == bt)
    kpos = jnp.arange(tk)
    mask = blk_ok[:, :, kpos // MOBA_BLOCK] & (kpos <= pos)
    logits = jnp.einsum('bhd,bshd->bhs', q, k) * HEAD_DIM ** -0.5 + tab[_t5_bucket(pos - kpos)].T[None]
    p = _masked_softmax(logits, mask)
    return jnp.einsum('bhs,bshd->bhd', p, v)


def _nsa_decode(q, g, pos, kc, vc, ks, vs, kw, vw, kpos0_w, pe, w1, w2, kn_cmp, tab):
    b, tk = kc.shape[0], kc.shape[1]
    n_cmp = (tk - NSA_CMP_LEN) // NSA_CMP_STRIDE + 1
    t16 = (n_cmp + 1) * NSA_CMP_STRIDE
    cmp_kv = _nsa_compress(kc[:, :t16], vc[:, :t16], pe, w1, w2, kn_cmp)
    kcmp, vcmp = cmp_kv[:, 0, :n_cmp], cmp_kv[:, 1, :n_cmp]
    cstart = jnp.arange(n_cmp) * NSA_CMP_STRIDE
    crel = pos - (cstart + NSA_CMP_LEN - 1)
    scale = HEAD_DIM ** -0.5
    lc = jnp.einsum('bhd,bnd->bhn', q, kcmp) * scale + tab[_t5_bucket(crel)].T[None]
    pc = _masked_softmax(lc, (crel >= 0)[None, None, :])
    oc = jnp.einsum('bhn,bnd->bhd', pc, vcmp)
    n_slc = -(-tk // NSA_SLC_BLOCK)
    jj = jnp.arange(n_slc)
    sstart = jj * NSA_SLC_BLOCK
    overlap = ((cstart[:, None] < sstart[None, :] + NSA_SLC_BLOCK)
               & (cstart[:, None] + NSA_CMP_LEN > sstart[None, :])).astype(F32)
    imp = jnp.einsum('bhn,nj->bj', pc, overlap)
    bt = pos // NSA_SLC_BLOCK
    forced = (jj == 0) | (jj == bt) | (jj == bt - 1)
    score = jnp.where(jj <= bt, jnp.where(forced, BIG, imp), -BIG)
    _, sel = lax.top_k(score, min(NSA_N_SEL, n_slc))
    blk_ok = jnp.any(sel[..., None] == jj, axis=-2)
    kpos = jnp.arange(tk)
    smask = blk_ok[:, kpos // NSA_SLC_BLOCK] & (kpos <= pos)
    ls = jnp.einsum('bhd,bsd->bhs', q, ks) * scale + tab[_t5_bucket(pos - kpos)].T[None]
    osel = jnp.einsum('bhs,bsd->bhd', _masked_softmax(ls, smask[:, None, :]), vs)
    wpos = kpos0_w + jnp.arange(kw.shape[1])
    wrel = pos - wpos
    lw = jnp.einsum('bhd,bsd->bhs', q, kw) * scale + tab[_t5_bucket(wrel)].T[None]
    ow = jnp.einsum('bhs,bsd->bhd', _masked_softmax(lw, ((wrel >= 0) & (wrel < NSA_WINDOW))[None, None, :]), vw)
    return g[..., 0:1] * oc + g[..., 1:2] * osel + g[..., 2:3] * ow


def _diff_decode(q, pos, k, v, lam, lam_init, out_gain, tab):
    kpos = jnp.arange(k.shape[1])
    logits = jnp.einsum('bhmd,bshmd->bmhs', q, k) * DIFF_DH ** -0.5 + tab[_t5_bucket(pos - kpos)].T[None, None]
    p = _masked_softmax(logits, (kpos <= pos)[None, None, None, :])
    a = p[:, 0] - lam * p[:, 1]
    o = jnp.einsum('bhs,bshd->bhd', a, v)
    return _rms(o, out_gain) * (1.0 - lam_init)


def _dsa_decode(q, qi, w, pos, k, v, ki, tab):
    b, tk = k.shape[0], k.shape[1]
    kpos = jnp.arange(tk)
    idx = jnp.einsum('bhd,bsd->bhs', qi, ki)
    score = jnp.einsum('bhs,bh->bs', jax.nn.relu(idx), w)
    score = jnp.where((kpos <= pos)[None], score, -BIG)
    _, sel = lax.top_k(score, min(DSA_TOPK, tk // 4))
    bi = jnp.arange(b)[:, None]
    kg, vg = k[bi, sel], v[bi, sel]
    rel = pos - sel
    logits = jnp.einsum('bhd,bsd->bhs', q, kg) * HEAD_DIM ** -0.5 + jnp.transpose(tab[_t5_bucket(rel)], (0, 2, 1))
    p = _masked_softmax(logits, (rel >= 0)[:, None, :])
    return jnp.einsum('bhs,bsd->bhd', p, vg)


def _layer_sample(x, pos, past, win_past, kpos0_w, lp, prep, rel_bias, lam_init):
    b = x.shape[0]
    w_main, g64, g32, w_gate, u_bf, vt_bf = prep
    (mq, nq, dq, sq, sqi, misc, moba_new, nsa_new, win_new, diff_new, dsa_new, _) = _in_proj(
        x.reshape(b, D_MODEL), lp['norm_mix'], w_main, g64, g32, b)
    f_moba = jnp.concatenate([past[0], moba_new.reshape(b, 1, 2, N_HEADS, HEAD_DIM)], axis=1)
    f_nsa = jnp.concatenate([past[1], nsa_new.reshape(b, 1, 4, HEAD_DIM)], axis=1)
    f_diff = jnp.concatenate([past[2], diff_new.reshape(b, 1, 2, N_HEADS, HEAD_DIM)], axis=1)
    f_dsa = jnp.concatenate([past[3], dsa_new.reshape(b, 1, DSA_ROW)], axis=1)
    wrows = jnp.concatenate([win_past, win_new.reshape(b, 1, 2, HEAD_DIM)], axis=1)
    tk = f_moba.shape[1]
    o_a = _moba_decode(mq.reshape(b, N_HEADS, HEAD_DIM), pos, f_moba[:, :, 0], f_moba[:, :, 1], rel_bias[:, 0:4])
    ng = jax.nn.sigmoid(misc[:, MISC_G:MISC_G + 3 * N_HEADS].reshape(b, N_HEADS, 3))
    o_b = _nsa_decode(nq.reshape(b, N_HEADS, HEAD_DIM), ng, pos, f_nsa[:, :, 0], f_nsa[:, :, 1], f_nsa[:, :, 2],
                      f_nsa[:, :, 3], wrows[:, :, 0], wrows[:, :, 1], kpos0_w, lp['nsa_pe'], lp['nsa_w1'],
                      lp['nsa_w2'], lp['qk_gain'][3], rel_bias[:, 4:8])
    dl = lp['diff_lambda'].astype(F32)
    lam = jnp.exp(jnp.sum(dl[0] * dl[1])) - jnp.exp(jnp.sum(dl[2] * dl[3])) + lam_init
    o_c = _diff_decode(dq.reshape(b, N_HEADS, 2, DIFF_DH), pos, f_diff[:, :, 0].reshape(b, tk, N_HEADS, 2, DIFF_DH),
                       f_diff[:, :, 1], lam, lam_init, lp['diff_out_gain'], rel_bias[:, 8:12])
    o_d = _dsa_decode(sq.reshape(b, N_HEADS, HEAD_DIM), sqi.reshape(b, DSA_IDX_HEADS, DSA_IDX_DIM),
                      misc[:, MISC_W:MISC_W + DSA_IDX_HEADS], pos, f_dsa[..., :HEAD_DIM],
                      f_dsa[..., HEAD_DIM:2 * HEAD_DIM], f_dsa[..., 2 * HEAD_DIM:], rel_bias[:, 12:16])
    f2 = lambda a: a.reshape(b, MIX_W)
    x1, hf_bf, qh = _merge(x.reshape(b, D_MODEL), lp['norm_mix'], w_gate, f2(o_a), f2(o_b), f2(o_c), f2(o_d),
                           lp['w_branch'], lp['w_out'], lp['norm_ffn'], lp['peer_wq'], b)
    x2 = _peer(hf_bf, qh, x1, lp['peer_subkeys'], u_bf, vt_bf, b)
    win_buf = win_past.shape[1]
    return (x2.reshape(b, 1, D_MODEL), moba_new.reshape(b, 1, 2, N_HEADS, HEAD_DIM), nsa_new.reshape(b, 1, 4, HEAD_DIM),
            diff_new.reshape(b, 1, 2, N_HEADS, HEAD_DIM), dsa_new.reshape(b, 1, DSA_ROW), wrows[:, -win_buf:])


def kernel(x_prompt, x_sample, cache_moba, cache_nsa, cache_diff, cache_dsa, state_nsa_win, page_table,
           rel_bias, norm_mix, w_in, qk_gain, nsa_pe, nsa_w1, nsa_w2, diff_qk_gain, diff_lambda,
           diff_out_gain, w_branch, w_out, norm_ffn, peer_wq, peer_subkeys, peer_u, peer_v):
    depth = w_in.shape[0]
    past_len = page_table.shape[1] * cache_moba.shape[2]
    win_buf = state_nsa_win.shape[2]
    assert x_sample.shape[1] == 1 and x_prompt.shape[1] >= win_buf and past_len >= win_buf

    def paged(cache_l):
        g = cache_l[page_table]
        return g.reshape((g.shape[0], g.shape[1] * g.shape[2]) + g.shape[3:])

    yp, ys = x_prompt, x_sample
    rows_p, rows_s = [], []
    for l in range(depth):
        lp = {'norm_mix': norm_mix[l], 'qk_gain': qk_gain[l], 'nsa_pe': nsa_pe[l],
              'nsa_w1': nsa_w1[l], 'nsa_w2': nsa_w2[l], 'diff_lambda': diff_lambda[l],
              'diff_out_gain': diff_out_gain[l], 'w_branch': w_branch[l],
              'w_out': w_out[l], 'norm_ffn': norm_ffn[l], 'peer_wq': peer_wq[l], 'peer_subkeys': peer_subkeys[l]}
        prep = _prep_w_in(w_in[l], qk_gain[l], diff_qk_gain[l]) + (peer_u[l].astype(BF16), peer_v[l].T.astype(BF16))
        lam_init = 0.8 - 0.6 * math.exp(-0.3 * l)
        out_p = _layer_prompt(yp, lp, prep, rel_bias, lam_init)
        past = (paged(cache_moba[l]), paged(cache_nsa[l]), paged(cache_diff[l]), paged(cache_dsa[l]))
        out_s = _layer_sample(ys, past_len, past, state_nsa_win[l], past_len - win_buf, lp, prep, rel_bias, lam_init)
        yp, ys = out_p[0], out_s[0]
        rows_p.append(out_p[1:5] + (out_p[5][:, -win_buf:],))
        rows_s.append(out_s[1:])

    def stk(rows, i):
        return jnp.stack([r[i] for r in rows], axis=0)

    return (yp, ys, stk(rows_p, 0), stk(rows_s, 0), stk(rows_p, 1), stk(rows_s, 1),
            stk(rows_p, 2), stk(rows_s, 2), stk(rows_p, 3), stk(rows_s, 3), stk(rows_p, 4), stk(rows_s, 4))
```

```python
import functools
import math

import jax
import jax.numpy as jnp
import numpy as np
from jax import lax
from jax.experimental import pallas as pl
from jax.experimental.pallas import tpu as pltpu

D_MODEL = 1024
N_MIXERS = 4
N_HEADS = 4
HEAD_DIM = 64
MIX_W = N_HEADS * HEAD_DIM
REL_BUCKETS = 32
REL_MAX_DIST = 128
MOBA_BLOCK = 256
MOBA_TOPK = 3
MOBA_QB = 32
NSA_CMP_LEN = 32
NSA_CMP_STRIDE = 16
NSA_CMP_HID = 128
NSA_SLC_BLOCK = 64
NSA_N_SEL = 16
NSA_WINDOW = 512
DIFF_DH = 32
DSA_IDX_HEADS = 4
DSA_IDX_DIM = 32
DSA_TOPK = 256
DSA_ROW = 2 * HEAD_DIM + DSA_IDX_DIM
PEER_HEADS = 8
PEER_KEYS = 128
PEER_TOPK = 16
PEER_QDIM = 256
PEER_CHUNK = 128
Q_BLOCK = 128
EPS = 1e-6
NEG = -1e30
BIG = 1e9

IN_SPLITS = (
    ('moba_q', MIX_W), ('moba_k', MIX_W), ('moba_v', MIX_W),
    ('nsa_q', MIX_W), ('nsa_kc', HEAD_DIM), ('nsa_vc', HEAD_DIM), ('nsa_ks', HEAD_DIM),
    ('nsa_vs', HEAD_DIM), ('nsa_kw', HEAD_DIM), ('nsa_vw', HEAD_DIM), ('nsa_g', 3 * N_HEADS),
    ('diff_q', N_HEADS * 2 * DIFF_DH), ('diff_k', N_HEADS * 2 * DIFF_DH), ('diff_v', N_HEADS * 2 * DIFF_DH),
    ('dsa_q', MIX_W), ('dsa_k', HEAD_DIM), ('dsa_v', HEAD_DIM),
    ('dsa_qi', DSA_IDX_HEADS * DSA_IDX_DIM), ('dsa_ki', DSA_IDX_DIM), ('dsa_w', DSA_IDX_HEADS),
    ('gates', N_MIXERS * D_MODEL),
)
IN_COLS = sum(s for _, s in IN_SPLITS)
IN_OFF = {}
_o = 0
for _n, _s in IN_SPLITS:
    IN_OFF[_n] = (_o, _s)
    _o += _s

LANE = 128
TILE = 256
VMEM_LIMIT = 56 * 1024 * 1024
F32 = jnp.float32
BF16 = jnp.bfloat16
INT_MIN = -2 ** 31

N64_W = 1280
N32_W = 512
RAW_W = 1152
MAIN_W = N64_W + N32_W + RAW_W
MISC_G = 0
MISC_W = 12


def _dot(a, b):
    return jnp.dot(a, b, preferred_element_type=F32)


def _dot_nt(a, b):
    return lax.dot_general(a, b, (((1,), (1,)), ((), ())), preferred_element_type=F32)


def _cparams(n_grid):
    return pltpu.CompilerParams(dimension_semantics=("arbitrary",) * n_grid, vmem_limit_bytes=VMEM_LIMIT)


def _t5_bucket(rel):
    n = jnp.maximum(rel, 0)
    exact = REL_BUCKETS // 2
    nf = jnp.maximum(n, 1).astype(jnp.float32)
    large = exact + (jnp.log(nf / exact) / math.log(REL_MAX_DIST / exact) * (REL_BUCKETS - exact)).astype(jnp.int32)
    return jnp.where(n < exact, n, jnp.minimum(large, REL_BUCKETS - 1))


def _bias_tiles(tab, tq, tk, n_d, window=None):
    r = jnp.arange(tq)[:, None]
    c = jnp.arange(tk)[None, :]
    rel = jnp.arange(n_d)[:, None, None] * tq + r - c
    ok = rel >= 0
    if window is not None:
        ok = ok & (rel < window)
    b = jnp.where(ok[..., None], tab[_t5_bucket(rel)], NEG)
    return jnp.transpose(b, (3, 0, 1, 2)).astype(F32)


def _prep_w_in(w_in, qk_gain, diff_qk_gain):
    def col(name):
        o, s = IN_OFF[name]
        return w_in[:, o:o + s]

    z = lambda n: jnp.zeros((D_MODEL, n), w_in.dtype)
    w_main = jnp.concatenate([
        col('moba_q'), col('moba_k'), col('nsa_q'), col('dsa_q'), col('nsa_ks'), col('nsa_kw'), col('dsa_k'), z(64),
        col('diff_q'), col('diff_k'),
        col('moba_v'), col('diff_v'), col('nsa_kc'), col('nsa_vc'), col('nsa_vs'), col('nsa_vw'), col('dsa_v'),
        col('dsa_ki'), z(32), col('dsa_qi'), col('nsa_g'), col('dsa_w'), z(LANE - 16)], axis=1).astype(BF16)
    t4 = lambda g: jnp.tile(g, N_HEADS)
    g64 = jnp.concatenate([t4(qk_gain[0]), t4(qk_gain[1]), t4(qk_gain[2]), t4(qk_gain[6]),
                           qk_gain[4], qk_gain[5], qk_gain[7], jnp.zeros((64,), F32)]).reshape(1, N64_W)
    g32 = jnp.concatenate([jnp.tile(diff_qk_gain[0], 8), jnp.tile(diff_qk_gain[1], 8)]).reshape(1, N32_W)
    w_gate = col('gates').astype(BF16)
    return w_main, g64, g32, w_gate


def _block_diag_ones(seg):
    i = np.arange(LANE)
    return jnp.asarray((i[:, None] // seg) == (i[None, :] // seg), BF16)


def _inproj_kernel(x_ref, g_ref, w_ref, g64_ref, g32_ref, bd64_ref, bd32_ref,
                   mq_ref, nq_ref, dq_ref, sq_ref, sqi_ref, misc_ref,
                   moba_ref, nsa_ref, win_ref, diff_ref, dsa_ref, kmean_ref):
    x = x_ref[...]
    hn = (x * lax.rsqrt(jnp.mean(x * x, axis=-1, keepdims=True) + EPS) * g_ref[...]).astype(BF16)

    def segnorm(p, bd, seg, gain):
        x2 = p * p
        hi = x2.astype(BF16)
        lo = (x2 - hi.astype(F32)).astype(BF16)
        ss = _dot(hi, bd) + _dot(lo, bd)
        return p * lax.rsqrt(ss * (1.0 / seg) + EPS) * gain

    p1 = _dot(hn, w_ref[:, 0:N64_W])
    bd64 = bd64_ref[...]
    c = [segnorm(p1[:, i * LANE:(i + 1) * LANE], bd64, 64, g64_ref[:, i * LANE:(i + 1) * LANE])
         for i in range(N64_W // LANE)]
    mq_ref[:, 0:128] = c[0]
    mq_ref[:, 128:256] = c[1]
    moba_ref[:, 0:128] = c[2]
    moba_ref[:, 128:256] = c[3]
    kmean_ref[0] = jnp.concatenate([jnp.mean(c[2], axis=0, keepdims=True),
                                    jnp.mean(c[3], axis=0, keepdims=True)], axis=1)
    nq_ref[:, 0:128] = c[4]
    nq_ref[:, 128:256] = c[5]
    sq_ref[:, 0:128] = c[6]
    sq_ref[:, 128:256] = c[7]
    nsa_ref[:, 128:192] = c[8][:, 0:64]
    win_ref[:, 0:64] = c[8][:, 64:128]
    dsa_ref[:, 0:64] = c[9][:, 0:64]

    p2 = _dot(hn, w_ref[:, N64_W:N64_W + N32_W])
    bd32 = bd32_ref[...]
    d = [segnorm(p2[:, i * LANE:(i + 1) * LANE], bd32, 32, g32_ref[:, i * LANE:(i + 1) * LANE])
         for i in range(N32_W // LANE)]
    dq_ref[:, 0:128] = d[0]
    dq_ref[:, 128:256] = d[1]
    diff_ref[:, 0:128] = d[2]
    diff_ref[:, 128:256] = d[3]

    p3 = _dot(hn, w_ref[:, N64_W + N32_W:MAIN_W])
    moba_ref[:, 256:512] = p3[:, 0:256]
    diff_ref[:, 256:512] = p3[:, 256:512]
    nsa_ref[:, 0:128] = p3[:, 512:640]
    nsa_ref[:, 192:256] = p3[:, 640:704]
    win_ref[:, 64:128] = p3[:, 704:768]
    dsa_ref[:, 64:128] = p3[:, 768:832]
    dsa_ref[:, 128:160] = p3[:, 832:864]
    sqi_ref[...] = p3[:, 896:1024]
    misc_ref[...] = p3[:, 1024:1152]


def _in_proj(x2d, norm_g, w_main, g64, g32, tm):
    n = x2d.shape[0]
    row = lambda w: pl.BlockSpec((tm, w), lambda i: (i, 0))
    full = lambda a: pl.BlockSpec(a.shape, lambda i: (0,) * a.ndim)
    bd64, bd32 = _block_diag_ones(64), _block_diag_ones(32)
    g = norm_g.reshape(1, D_MODEL)
    widths = (MIX_W, MIX_W, MIX_W, MIX_W, 128, 128, 512, 256, 128, 512, DSA_ROW)
    out_shape = [jax.ShapeDtypeStruct((n, w), F32) for w in widths]
    out_shape.append(jax.ShapeDtypeStruct((n // tm, 1, MIX_W), F32))
    out_specs = [row(w) for w in widths] + [pl.BlockSpec((1, 1, MIX_W), lambda i: (i, 0, 0))]
    return pl.pallas_call(
        _inproj_kernel,
        grid=(n // tm,),
        in_specs=[row(D_MODEL), full(g), full(w_main), full(g64), full(g32), full(bd64), full(bd32)],
        out_specs=out_specs,
        out_shape=out_shape,
        compiler_params=_cparams(1),
    )(x2d, g, w_main, g64, g32, bd64, bd32)


def _flash_init(m_sc, l_sc, acc_sc):
    m_sc[...] = jnp.full(m_sc.shape, NEG, F32)
    l_sc[...] = jnp.zeros(l_sc.shape, F32)
    acc_sc[...] = jnp.zeros(acc_sc.shape, F32)


def _flash_update_t(st, i, vt_bf, m_sc, l_sc, acc_sc):
    m_old = m_sc[i]
    m_new = jnp.maximum(m_old, jnp.max(st, axis=0, keepdims=True))
    alpha = jnp.exp(m_old - m_new)
    p = jnp.exp(st - m_new)
    l_sc[i] = alpha * l_sc[i] + jnp.sum(p, axis=0, keepdims=True)
    acc_sc[i] = alpha * acc_sc[i] + _dot(vt_bf, p.astype(BF16))
    m_sc[i] = m_new


def _flash_scratch_t(n_streams, tq, dv):
    return [pltpu.VMEM((n_streams, 1, tq), F32), pltpu.VMEM((n_streams, 1, tq), F32),
            pltpu.VMEM((n_streams, dv, tq), F32)]


def _bias_tiles_t(tab, tq, n_d, window=None):
    return jnp.swapaxes(_bias_tiles(tab, tq, tq, n_d, window), 2, 3)


def _tiles_t(a, tq):
    b, t, w = a.shape
    return jnp.transpose(a.reshape(b, t // tq, tq, w), (0, 1, 3, 2)).astype(BF16)


def _diff_kernel(q_ref, k_ref, v_ref, bt_ref, dl_ref, gain_ref, o_ref, m_sc, l_sc, acc_sc, *, lam_init):
    qi = pl.program_id(1)
    tq = q_ref.shape[1]
    scale = DIFF_DH ** -0.5
    q = q_ref[0].astype(BF16)
    _flash_init(m_sc, l_sc, acc_sc)

    def body(kj, carry):
        d = jnp.minimum(qi - kj, 2)
        off = pl.multiple_of(kj * tq, tq)
        k = k_ref[0, pl.ds(off, tq), :]
        vt = v_ref[0, kj]
        for h in range(N_HEADS):
            bias = bt_ref[h, d]
            vh = vt[h * HEAD_DIM:(h + 1) * HEAD_DIM, :]
            for m in range(2):
                lo = h * HEAD_DIM + m * DIFF_DH
                st = _dot_nt(k[:, lo:lo + DIFF_DH], q[:, lo:lo + DIFF_DH]) * scale + bias
                _flash_update_t(st, 2 * h + m, vh, m_sc, l_sc, acc_sc)
        return carry

    lax.fori_loop(0, qi + 1, body, 0)
    dl = dl_ref[...]
    lam = (jnp.exp(jnp.sum(dl[0:1] * dl[1:2], axis=-1, keepdims=True))
           - jnp.exp(jnp.sum(dl[2:3] * dl[3:4], axis=-1, keepdims=True)) + lam_init)
    outs = []
    for h in range(N_HEADS):
        o = acc_sc[2 * h] / l_sc[2 * h] - lam * (acc_sc[2 * h + 1] / l_sc[2 * h + 1])
        outs.append(o * lax.rsqrt(jnp.mean(o * o, axis=0, keepdims=True) + EPS) * gain_ref[...] * (1.0 - lam_init))
    o_ref[0] = jnp.concatenate(outs, axis=0).T


def _diff_prompt(dq, dk_bf, dv_t, tab, dl, out_gain, lam_init):
    b, t, _ = dq.shape
    tq = TILE
    bt = _bias_tiles_t(tab, tq, 3)
    gain = out_gain.reshape(HEAD_DIM, 1)
    full = lambda a: pl.BlockSpec(a.shape, lambda i, j: (0,) * a.ndim)
    return pl.pallas_call(
        functools.partial(_diff_kernel, lam_init=lam_init),
        grid=(b, t // tq),
        in_specs=[pl.BlockSpec((1, tq, MIX_W), lambda i, j: (i, j, 0)),
                  pl.BlockSpec((1, t, MIX_W), lambda i, j: (i, 0, 0)),
                  pl.BlockSpec((1, t // tq, MIX_W, tq), lambda i, j: (i, 0, 0, 0)),
                  full(bt), full(dl), full(gain)],
        out_specs=pl.BlockSpec((1, tq, MIX_W), lambda i, j: (i, j, 0)),
        out_shape=jax.ShapeDtypeStruct((b, t, MIX_W), F32),
        scratch_shapes=_flash_scratch_t(2 * N_HEADS, tq, HEAD_DIM),
        compiler_params=_cparams(2),
    )(dq, dk_bf, dv_t, bt, dl, gain)


def _moba_kernel(q_ref, k_ref, v_ref, km_ref, bt_ref, o_ref, sel_sc, m_sc, l_sc, acc_sc):
    qi = pl.program_id(1)
    tq = q_ref.shape[1]
    scale = HEAD_DIM ** -0.5
    q = q_ref[0].astype(BF16)
    km = km_ref[0].astype(BF16)
    jidx = lax.broadcasted_iota(jnp.int32, (LANE, tq), 0)
    _flash_init(m_sc, l_sc, acc_sc)
    for h in range(N_HEADS):
        hs = slice(h * HEAD_DIM, (h + 1) * HEAD_DIM)
        g = _dot_nt(km[:, hs], q[:, hs])
        g = jnp.where(jidx < qi, g, -BIG)
        sel = jidx == qi
        for _ in range(MOBA_TOPK):
            mx = jnp.max(g, axis=0, keepdims=True)
            first = jnp.min(jnp.where(g == mx, jidx, LANE), axis=0, keepdims=True)
            hit = jidx == first
            sel = sel | (hit & (mx > -0.5 * BIG))
            g = jnp.where(hit, -3.0 * BIG, g)
        sel_sc[h] = sel.astype(F32)

    def body(kj, carry):
        d = jnp.minimum(qi - kj, 2)
        off = pl.multiple_of(kj * tq, tq)
        k = k_ref[0, pl.ds(off, tq), :]
        vt = v_ref[0, kj]
        for h in range(N_HEADS):
            hs = slice(h * HEAD_DIM, (h + 1) * HEAD_DIM)
            colsel = sel_sc[h, pl.ds(kj, 1), :] > 0.5
            st = _dot_nt(k[:, hs], q[:, hs]) * scale + bt_ref[h, d]
            st = jnp.where(colsel, st, NEG)
            _flash_update_t(st, h, vt[hs, :], m_sc, l_sc, acc_sc)
        return carry

    lax.fori_loop(0, qi + 1, body, 0)
    o_ref[0] = jnp.concatenate([acc_sc[h] / l_sc[h] for h in range(N_HEADS)], axis=0).T


def _moba_prompt(mq, k_bf, v_t, kmean, tab):
    b, t, _ = mq.shape
    tq = TILE
    assert tq == MOBA_BLOCK and t % tq == 0 and t // tq <= LANE
    bt = _bias_tiles_t(tab, tq, 3)
    km = jnp.pad(kmean, ((0, 0), (0, LANE - kmean.shape[1]), (0, 0)))
    full = lambda a: pl.BlockSpec(a.shape, lambda i, j: (0,) * a.ndim)
    return pl.pallas_call(
        _moba_kernel,
        grid=(b, t // tq),
        in_specs=[pl.BlockSpec((1, tq, MIX_W), lambda i, j: (i, j, 0)),
                  pl.BlockSpec((1, t, MIX_W), lambda i, j: (i, 0, 0)),
                  pl.BlockSpec((1, t // tq, MIX_W, tq), lambda i, j: (i, 0, 0, 0)),
                  pl.BlockSpec((1, LANE, MIX_W), lambda i, j: (i, 0, 0)),
                  full(bt)],
        out_specs=pl.BlockSpec((1, tq, MIX_W), lambda i, j: (i, j, 0)),
        out_shape=jax.ShapeDtypeStruct((b, t, MIX_W), F32),
        scratch_shapes=[pltpu.VMEM((N_HEADS, LANE, tq), F32)] + _flash_scratch_t(N_HEADS, tq, HEAD_DIM),
        compiler_params=_cparams(2),
    )(mq, k_bf, v_t, km, bt)


def _sortable(x):
    x = jnp.where(x == 0.0, 0.0, x)
    bits = pltpu.bitcast(x, jnp.int32)
    return bits ^ ((bits >> 31) & 0x7FFFFFFF)


def _dsa_kernel(q_ref, qi_ref, misc_ref, k_ref, v_ref, ki_ref, bt_ref, tri_ref, o_ref,
                key_sc, m_sc, l_sc, acc_sc):
    qi = pl.program_id(1)
    tq = q_ref.shape[1]
    scale = HEAD_DIM ** -0.5
    q = q_ref[0].astype(BF16)
    qidx = qi_ref[0].astype(BF16)
    wt = misc_ref[0].T[MISC_W:MISC_W + DSA_IDX_HEADS, :]
    k_io = lax.broadcasted_iota(jnp.int32, (tq, tq), 0)
    q_io = lax.broadcasted_iota(jnp.int32, (tq, tq), 1)
    nk = qi + 1

    def score_body(kj, carry):
        off = pl.multiple_of(kj * tq, tq)
        ki = ki_ref[0, pl.ds(off, tq), :]
        sc = jnp.zeros((tq, tq), F32)
        for h in range(DSA_IDX_HEADS):
            idx = _dot_nt(ki, qidx[:, h * DSA_IDX_DIM:(h + 1) * DSA_IDX_DIM])
            sc = sc + jnp.maximum(idx, 0.0) * wt[h:h + 1, :]
        sc = jnp.where((kj < qi) | (k_io <= q_io), sc, -BIG)
        key_sc[kj] = _sortable(sc)
        return carry

    lax.fori_loop(0, nk, score_body, 0)

    def count_ge(tau):
        def cb(kj, cnt):
            return cnt + jnp.sum(jnp.where(key_sc[kj] >= tau, 1.0, 0.0), axis=0, keepdims=True)
        return lax.fori_loop(0, nk, cb, jnp.zeros((1, tq), F32))

    def bis_body(it, tau):
        cand = tau + lax.shift_left(jnp.int32(1), 31 - it)
        return jnp.where(count_ge(cand) >= float(DSA_TOPK), cand, tau)

    tau = lax.fori_loop(0, 32, bis_body, jnp.full((1, tq), INT_MIN, jnp.int32))
    n_gt = count_ge(tau + 1)
    need = float(DSA_TOPK) - n_gt
    tri = tri_ref[...]
    _flash_init(m_sc, l_sc, acc_sc)

    def att_body(kj, run):
        d = jnp.minimum(qi - kj, 2)
        off = pl.multiple_of(kj * tq, tq)
        key = key_sc[kj]
        eq = key == tau
        pref = run + _dot(tri, jnp.where(eq, 1.0, 0.0).astype(BF16))
        mask = (key > tau) | (eq & (pref <= need))
        k = k_ref[0, pl.ds(off, tq), :]
        vt = v_ref[0, kj]
        for h in range(N_HEADS):
            st = _dot_nt(k, q[:, h * HEAD_DIM:(h + 1) * HEAD_DIM]) * scale + bt_ref[h, d]
            st = jnp.where(mask, st, NEG)
            _flash_update_t(st, h, vt, m_sc, l_sc, acc_sc)
        return run + jnp.sum(jnp.where(eq, 1.0, 0.0), axis=0, keepdims=True)

    lax.fori_loop(0, nk, att_body, jnp.zeros((1, tq), F32))
    o_ref[0] = jnp.concatenate([acc_sc[h] / l_sc[h] for h in range(N_HEADS)], axis=0).T


def _dsa_prompt(sq, sqi, misc, k_bf, v_t, ki_bf, tab):
    b, t, _ = sq.shape
    tq = TILE
    assert tq >= DSA_TOPK and t // 4 >= DSA_TOPK
    bt = _bias_tiles_t(tab, tq, 3)
    i = np.arange(tq)
    tri = jnp.asarray(i[:, None] >= i[None, :], BF16)
    full = lambda a: pl.BlockSpec(a.shape, lambda i, j: (0,) * a.ndim)
    seq = lambda w: pl.BlockSpec((1, t, w), lambda i, j: (i, 0, 0))
    til = lambda w: pl.BlockSpec((1, tq, w), lambda i, j: (i, j, 0))
    return pl.pallas_call(
        _dsa_kernel,
        grid=(b, t // tq),
        in_specs=[til(MIX_W), til(128), til(128), seq(HEAD_DIM),
                  pl.BlockSpec((1, t // tq, HEAD_DIM, tq), lambda i, j: (i, 0, 0, 0)),
                  seq(DSA_IDX_DIM), full(bt), full(tri)],
        out_specs=til(MIX_W),
        out_shape=jax.ShapeDtypeStruct((b, t, MIX_W), F32),
        scratch_shapes=[pltpu.VMEM((t // tq, tq, tq), jnp.int32)] + _flash_scratch_t(N_HEADS, tq, HEAD_DIM),
        compiler_params=_cparams(2),
    )(sq, sqi, misc, k_bf, v_t, ki_bf, bt, tri)


def _compress_kernel(c_ref, pe_ref, w1_ref, w2_ref, g_ref, o_ref):
    for i in range(2):
        o_ref[0, i] = _compress_math(c_ref[0, i], i, pe_ref, w1_ref, w2_ref, g_ref)


def _nsa_compress(kc, vc, pe, w1, w2, kn_gain):
    b, tc, _ = kc.shape
    nch = tc // NSA_CMP_STRIDE
    c = jnp.stack([kc, vc], axis=1).reshape(b, 2, nch, NSA_CMP_STRIDE * HEAD_DIM)
    pe2 = pe.reshape(2, 1, NSA_CMP_LEN * HEAD_DIM)
    full = lambda a: pl.BlockSpec(a.shape, lambda i: (0,) * a.ndim)
    g = kn_gain.reshape(1, HEAD_DIM)
    w1b, w2b = w1.astype(BF16), w2.astype(BF16)
    return pl.pallas_call(
        _compress_kernel,
        grid=(b,),
        in_specs=[pl.BlockSpec((1, 2, nch, NSA_CMP_STRIDE * HEAD_DIM), lambda i: (i, 0, 0, 0)),
                  full(pe2), full(w1b), full(w2b), full(g)],
        out_specs=pl.BlockSpec((1, 2, nch, HEAD_DIM), lambda i: (i, 0, 0, 0)),
        out_shape=jax.ShapeDtypeStruct((b, 2, nch, HEAD_DIM), F32),
        compiler_params=_cparams(1),
    )(c, pe2, w1b, w2b, g)


def _nsa_kernel(q_ref, misc_ref, kcmp_ref, vcmpt_ref, ks_ref, vs_ref, kw_ref, vw_ref, cb_ref, ov_ref, ex_ref,
                bt_ref, wbt_ref, o_ref, m_sc, l_sc, acc_sc, oc_sc):
    qi = pl.program_id(1)
    tq = q_ref.shape[1]
    n_slc = ov_ref.shape[0]
    scale = HEAD_DIM ** -0.5
    q = q_ref[0].astype(BF16)
    kcmp = kcmp_ref[0].astype(BF16)
    vcmpt = vcmpt_ref[0].astype(BF16)
    ovt = ov_ref[...]

    imp = jnp.zeros((n_slc, tq), F32)
    for h in range(N_HEADS):
        cb = cb_ref[h]
        ok = cb > 0.5 * NEG
        z = jnp.where(ok, _dot_nt(kcmp, q[:, h * HEAD_DIM:(h + 1) * HEAD_DIM]) * scale + cb, NEG)
        z = z - jnp.max(z, axis=0, keepdims=True)
        p = jnp.where(ok, jnp.exp(z), 0.0)
        p = p / jnp.maximum(jnp.sum(p, axis=0, keepdims=True), 1e-30)
        pb = p.astype(BF16)
        oc_sc[h] = _dot(vcmpt, pb)
        imp = imp + _dot(ovt, pb)

    jidx = lax.broadcasted_iota(jnp.int32, (n_slc, tq), 0)
    pos = qi * tq + lax.broadcasted_iota(jnp.int32, (n_slc, tq), 1)
    bt = pos // NSA_SLC_BLOCK
    forced = (jidx == 0) | (jidx == bt) | (jidx == bt - 1)
    score = jnp.where(jidx <= bt, jnp.where(forced, BIG, imp), -BIG)
    sel = jnp.zeros((n_slc, tq), F32)
    for _ in range(min(NSA_N_SEL, n_slc)):
        mx = jnp.max(score, axis=0, keepdims=True)
        first = jnp.min(jnp.where(score == mx, jidx, n_slc), axis=0, keepdims=True)
        hit = jidx == first
        sel = jnp.where(hit, 1.0, sel)
        score = jnp.where(hit, -3.0 * BIG, score)
    sel_bf = sel.astype(BF16)

    _flash_init(m_sc, l_sc, acc_sc)

    def slc_body(kj, carry):
        d = jnp.minimum(qi - kj, 2)
        off = pl.multiple_of(kj * tq, tq)
        mask = _dot(ex_ref[kj], sel_bf) > 0.5
        k = ks_ref[0, pl.ds(off, tq), :]
        vt = vs_ref[0, kj]
        for h in range(N_HEADS):
            st = _dot_nt(k, q[:, h * HEAD_DIM:(h + 1) * HEAD_DIM]) * scale + bt_ref[h, d]
            st = jnp.where(mask, st, NEG)
            _flash_update_t(st, h, vt, m_sc, l_sc, acc_sc)
        return carry

    lax.fori_loop(0, qi + 1, slc_body, 0)

    n_w = wbt_ref.shape[1]

    def win_body(kj, carry):
        d = qi - kj
        off = pl.multiple_of(kj * tq, tq)
        k = kw_ref[0, pl.ds(off, tq), :]
        vt = vw_ref[0, kj]
        for h in range(N_HEADS):
            st = _dot_nt(k, q[:, h * HEAD_DIM:(h + 1) * HEAD_DIM]) * scale + wbt_ref[h, d]
            _flash_update_t(st, N_HEADS + h, vt, m_sc, l_sc, acc_sc)
        return carry

    lax.fori_loop(jnp.maximum(qi - (n_w - 1), 0), qi + 1, win_body, 0)

    gl = misc_ref[0].T[MISC_G:MISC_G + 3 * N_HEADS, :]
    g = 1.0 / (1.0 + jnp.exp(-gl))
    outs = []
    for h in range(N_HEADS):
        outs.append(g[3 * h:3 * h + 1] * oc_sc[h]
                    + g[3 * h + 1:3 * h + 2] * (acc_sc[h] / l_sc[h])
                    + g[3 * h + 2:3 * h + 3] * (acc_sc[N_HEADS + h] / l_sc[N_HEADS + h]))
    o_ref[0] = jnp.concatenate(outs, axis=0).T


def _nsa_prompt(nq, misc, cmp_kv, ks_bf, vs_t, kw_bf, vw_t, tab):
    b, t, _ = nq.shape
    tq = TILE
    n_cmp = (t - NSA_CMP_LEN) // NSA_CMP_STRIDE + 1
    n_cpad = cmp_kv.shape[2]
    n_slc = t // NSA_SLC_BLOCK
    assert t % tq == 0 and tq % NSA_SLC_BLOCK == 0 and n_cpad >= n_cmp
    kcmp = cmp_kv[:, 0]
    vcmpt = jnp.swapaxes(cmp_kv[:, 1], 1, 2)
    cidx = jnp.arange(n_cpad)
    crel = jnp.arange(t)[None, :] - (cidx * NSA_CMP_STRIDE + NSA_CMP_LEN - 1)[:, None]
    cok = (crel >= 0) & (cidx < n_cmp)[:, None]
    cbias = jnp.transpose(jnp.where(cok[..., None], tab[_t5_bucket(crel)], NEG), (2, 0, 1)).astype(F32)
    cstart = cidx * NSA_CMP_STRIDE
    sstart = jnp.arange(n_slc) * NSA_SLC_BLOCK
    overlap_t = ((cstart[None, :] < sstart[:, None] + NSA_SLC_BLOCK)
                 & (cstart[None, :] + NSA_CMP_LEN > sstart[:, None]) & (cidx < n_cmp)[None, :]).astype(BF16)
    kpos = jnp.arange(t).reshape(t // tq, tq, 1)
    expand_t = (kpos // NSA_SLC_BLOCK == jnp.arange(n_slc)[None, None, :]).astype(BF16)
    bt = _bias_tiles_t(tab, tq, 3)
    n_w = (NSA_WINDOW - 1 + tq - 1) // tq + 1
    wbt = _bias_tiles_t(tab, tq, n_w, window=NSA_WINDOW)
    full = lambda a: pl.BlockSpec(a.shape, lambda i, j: (0,) * a.ndim)
    seq = lambda w: pl.BlockSpec((1, t, w), lambda i, j: (i, 0, 0))
    seq_t = pl.BlockSpec((1, t // tq, HEAD_DIM, tq), lambda i, j: (i, 0, 0, 0))
    til = lambda w: pl.BlockSpec((1, tq, w), lambda i, j: (i, j, 0))
    return pl.pallas_call(
        _nsa_kernel,
        grid=(b, t // tq),
        in_specs=[til(MIX_W), til(128),
                  pl.BlockSpec((1, n_cpad, HEAD_DIM), lambda i, j: (i, 0, 0)),
                  pl.BlockSpec((1, HEAD_DIM, n_cpad), lambda i, j: (i, 0, 0)),
                  seq(HEAD_DIM), seq_t, seq(HEAD_DIM), seq_t,
                  pl.BlockSpec((N_HEADS, n_cpad, tq), lambda i, j: (0, 0, j)),
                  full(overlap_t), full(expand_t), full(bt), full(wbt)],
        out_specs=til(MIX_W),
        out_shape=jax.ShapeDtypeStruct((b, t, MIX_W), F32),
        scratch_shapes=_flash_scratch_t(2 * N_HEADS, tq, HEAD_DIM) + [pltpu.VMEM((N_HEADS, HEAD_DIM, tq), F32)],
        compiler_params=_cparams(2),
    )(nq, misc, kcmp, vcmpt, ks_bf, vs_t, kw_bf, vw_t, cbias, overlap_t, expand_t, bt, wbt)


def _merge_kernel(x_ref, g_ref, wg_ref, oa_ref, ob_ref, oc_ref, od_ref, wb_ref, wo_ref, gf_ref, wq_ref,
                  x1_ref, hf_ref, qh_ref):
    x = x_ref[...]
    hn = (x * lax.rsqrt(jnp.mean(x * x, axis=-1, keepdims=True) + EPS) * g_ref[...]).astype(BF16)
    merged = jnp.zeros(x.shape, F32)
    for i, o_ref in enumerate((oa_ref, ob_ref, oc_ref, od_ref)):
        gl = _dot(hn, wg_ref[:, i * D_MODEL:(i + 1) * D_MODEL])
        z = _dot(o_ref[...].astype(BF16), wb_ref[i])
        merged = merged + z / (1.0 + jnp.exp(-gl))
    x1 = x + _dot(merged.astype(BF16), wo_ref[...])
    x1_ref[...] = x1
    hf = (x1 * lax.rsqrt(jnp.mean(x1 * x1, axis=-1, keepdims=True) + EPS) * gf_ref[...])
    hfb = hf.astype(BF16)
    hf_ref[...] = hfb
    qh_ref[...] = _dot(hfb, wq_ref[...])


def _merge(x2d, norm_g, w_gate, o_a, o_b, o_c, o_d, w_branch, w_out, norm_ffn, wq, tm):
    n = x2d.shape[0]
    row = lambda w: pl.BlockSpec((tm, w), lambda i: (i, 0))
    full = lambda a: pl.BlockSpec(a.shape, lambda i: (0,) * a.ndim, pipeline_mode=pl.Buffered(1))
    g = norm_g.reshape(1, D_MODEL)
    gf = norm_ffn.reshape(1, D_MODEL)
    wb = w_branch.astype(BF16)
    wo = w_out.astype(BF16)
    wqb = wq.astype(BF16)
    nq = wq.shape[1]
    return pl.pallas_call(
        _merge_kernel,
        grid=(n // tm,),
        in_specs=[row(D_MODEL), full(g), full(w_gate), row(MIX_W), row(MIX_W), row(MIX_W), row(MIX_W),
                  full(wb), full(wo), full(gf), full(wqb)],
        out_specs=[row(D_MODEL), row(D_MODEL), row(nq)],
        out_shape=[jax.ShapeDtypeStruct((n, D_MODEL), F32), jax.ShapeDtypeStruct((n, D_MODEL), BF16),
                   jax.ShapeDtypeStruct((n, nq), F32)],
        compiler_params=_cparams(1),
    )(x2d, g, w_gate, o_a, o_b, o_c, o_d, wb, wo, gf, wqb)


PEER_I1_CHUNK = 8
PEER_ECHUNK = PEER_I1_CHUNK * PEER_KEYS


def _top_desc(x, n, iota0):
    rows = []
    big = x.shape[0]
    for _ in range(n):
        mx = jnp.max(x, axis=0, keepdims=True)
        first = jnp.min(jnp.where(x == mx, iota0, big), axis=0, keepdims=True)
        x = jnp.where(iota0 == first, -jnp.inf, x)
        rows.append(mx)
    return rows


def _peer_kernel(hf_ref, qh_ref, sk_ref, u_ref, vt_ref, x1_ref, o_ref, s1_sc, s2_sc, e1_sc, e2_sc, tau_sc, acc_sc):
    c = pl.program_id(1)
    tm = hf_ref.shape[0]
    half = PEER_QDIM // 2

    @pl.when(c == 0)
    def _():
        acc_sc[...] = jnp.zeros(acc_sc.shape, F32)
        io_k = lax.broadcasted_iota(jnp.int32, (PEER_KEYS, tm), 0)
        n_cand = sum(PEER_TOPK // (a + 1) for a in range(PEER_TOPK))
        n_cand_pad = -n_cand % 8
        io_c = lax.broadcasted_iota(jnp.int32, (n_cand + n_cand_pad, tm), 0)
        for h in range(PEER_HEADS):
            qh = qh_ref[:, h * PEER_QDIM:(h + 1) * PEER_QDIM].astype(BF16)
            s1 = _dot_nt(sk_ref[0, h], qh[:, 0:half])
            s2 = _dot_nt(sk_ref[1, h], qh[:, half:PEER_QDIM])
            t1 = _top_desc(s1, PEER_TOPK, io_k)
            t2 = jnp.concatenate(_top_desc(s2, PEER_TOPK, io_k), axis=0)
            cand = jnp.concatenate([t1[a] + t2[0:PEER_TOPK // (a + 1)] for a in range(PEER_TOPK)]
                                   + [jnp.full((n_cand_pad, tm), -jnp.inf, F32)], axis=0)
            ts = _top_desc(cand, PEER_TOPK, io_c)
            zsum = ts[0] * 0.0
            for a in range(PEER_TOPK):
                zsum = zsum + jnp.exp(ts[a] - ts[0])
            m1 = t1[0]
            m2 = t2[0:1]
            s1_sc[h] = s1
            s2_sc[h] = s2
            e1_sc[h] = jnp.exp(s1 - m1) / zsum
            e2_sc[h] = jnp.exp(s2 - m2)
            tau_sc[h] = jnp.broadcast_to(ts[PEER_TOPK - 1], (8, tm))

    act = jax.nn.gelu(_dot_nt(u_ref[...], hf_ref[...]))
    for j in range(PEER_I1_CHUNK):
        i1 = c * PEER_I1_CHUNK + j
        wj = jnp.zeros((PEER_KEYS, tm), F32)
        for h in range(PEER_HEADS):
            tot = s1_sc[h, pl.ds(i1, 1), :] + s2_sc[h]
            wj = wj + jnp.where(tot >= tau_sc[h, 0:1, :], e1_sc[h, pl.ds(i1, 1), :] * e2_sc[h], 0.0)
        gj = (wj * act[j * PEER_KEYS:(j + 1) * PEER_KEYS, :]).astype(BF16)
        acc_sc[...] += _dot(vt_ref[:, j * PEER_KEYS:(j + 1) * PEER_KEYS], gj)

    @pl.when(c == pl.num_programs(1) - 1)
    def _():
        o_ref[...] = x1_ref[...] + acc_sc[...].T


def _peer(hf_bf, qh, x1, subkeys, u_bf, vt_bf, tm):
    n = hf_bf.shape[0]
    n_exp = u_bf.shape[0]
    sk = subkeys.astype(BF16)
    nc = n_exp // PEER_ECHUNK
    return pl.pallas_call(
        _peer_kernel,
        grid=(n // tm, nc),
        in_specs=[pl.BlockSpec((tm, D_MODEL), lambda i, c: (i, 0)),
                  pl.BlockSpec((tm, PEER_HEADS * PEER_QDIM), lambda i, c: (i, 0)),
                  pl.BlockSpec(sk.shape, lambda i, c: (0, 0, 0, 0)),
                  pl.BlockSpec((PEER_ECHUNK, D_MODEL), lambda i, c: (c, 0)),
                  pl.BlockSpec((D_MODEL, PEER_ECHUNK), lambda i, c: (0, c)),
                  pl.BlockSpec((tm, D_MODEL), lambda i, c: (i, 0))],
        out_specs=pl.BlockSpec((tm, D_MODEL), lambda i, c: (i, 0)),
        out_shape=jax.ShapeDtypeStruct((n, D_MODEL), F32),
        scratch_shapes=[pltpu.VMEM((PEER_HEADS, PEER_KEYS, tm), F32)] * 4
        + [pltpu.VMEM((PEER_HEADS, 8, tm), F32), pltpu.VMEM((D_MODEL, tm), F32)],
        compiler_params=_cparams(2),
    )(hf_bf, qh, sk, u_bf, vt_bf, x1)


def _layer_prompt(x, lp, prep, rel_bias, lam_init):
    b, t, _ = x.shape
    n = b * t
    w_main, g64, g32, w_gate, u_bf, vt_bf = prep
    (mq, nq, dq, sq, sqi, misc, moba_new, nsa_new, win_new, diff_new, dsa_new, kmean) = _in_proj(
        x.reshape(n, D_MODEL), lp['norm_mix'], w_main, g64, g32, TILE)
    r3 = lambda a: a.reshape(b, t, a.shape[-1])
    moba3, nsa3, win3, diff3, dsa3 = r3(moba_new), r3(nsa_new), r3(win_new), r3(diff_new), r3(dsa_new)
    bf = lambda a: a.astype(BF16)
    vt = lambda a: _tiles_t(a, TILE)
    o_a = _moba_prompt(r3(mq), bf(moba3[..., :MIX_W]), vt(moba3[..., MIX_W:]),
                       kmean.reshape(b, t // TILE, MIX_W), rel_bias[:, 0:4])
    t16 = t // NSA_CMP_STRIDE * NSA_CMP_STRIDE
    cmp_kv = _nsa_compress(nsa3[:, :t16, 0:64], nsa3[:, :t16, 64:128], lp['nsa_pe'], lp['nsa_w1'], lp['nsa_w2'],
                           lp['qk_gain'][3])
    o_b = _nsa_prompt(r3(nq), r3(misc), cmp_kv, bf(nsa3[..., 128:192]), vt(nsa3[..., 192:256]),
                      bf(win3[..., 0:64]), vt(win3[..., 64:128]), rel_bias[:, 4:8])
    o_c = _diff_prompt(r3(dq), bf(diff3[..., :MIX_W]), vt(diff3[..., MIX_W:]), rel_bias[:, 8:12],
                       lp['diff_lambda'].astype(F32), lp['diff_out_gain'], lam_init)
    o_d = _dsa_prompt(r3(sq), r3(sqi), r3(misc), bf(dsa3[..., 0:64]), vt(dsa3[..., 64:128]), bf(dsa3[..., 128:160]),
                      rel_bias[:, 12:16])
    f2 = lambda a: a.reshape(n, MIX_W)
    x1, hf_bf, qh = _merge(x.reshape(n, D_MODEL), lp['norm_mix'], w_gate, f2(o_a), f2(o_b), f2(o_c), f2(o_d),
                           lp['w_branch'], lp['w_out'], lp['norm_ffn'], lp['peer_wq'], TILE)
    x2 = _peer(hf_bf, qh, x1, lp['peer_subkeys'], u_bf, vt_bf, 512)
    return (x2.reshape(b, t, D_MODEL), moba3.reshape(b, t, 2, N_HEADS, HEAD_DIM), nsa3.reshape(b, t, 4, HEAD_DIM),
            diff3.reshape(b, t, 2, N_HEADS, HEAD_DIM), dsa3, win3.reshape(b, t, 2, HEAD_DIM))


ROWS = 8


def _fetch_pages(pt_ref, cache_ref, bufs, sems, layer, n_pages, page):
    b = pl.program_id(0)
    copies = []
    for j in range(n_pages):
        src = cache_ref.at[layer, pt_ref[b * n_pages + j]]
        for i, (buf, sem) in enumerate(zip(bufs, sems)):
            w = buf.shape[1]
            part = src if len(bufs) == 1 else src.at[:, pl.ds(i * w, w)]
            cp = pltpu.make_async_copy(part, buf.at[pl.ds(j * page, page), :], sem.at[j])
            cp.start()
            copies.append(cp)
    for cp in copies:
        cp.wait()


def _row_dot(qrows_bf, new_row):
    return jnp.sum(qrows_bf.astype(F32) * new_row.astype(BF16).astype(F32), axis=-1, keepdims=True)


def _softmax_av(s, s_new, v_bf, v_new):
    m = jnp.maximum(jnp.max(s, axis=-1, keepdims=True), s_new)
    p = jnp.exp(s - m)
    p_new = jnp.exp(s_new - m)
    den = jnp.sum(p, axis=-1, keepdims=True) + p_new
    num = _dot(p.astype(BF16), v_bf) + p_new.astype(BF16).astype(F32) * v_new.astype(BF16).astype(F32)
    return num / jnp.maximum(den, 1e-30)


def _top_lanes(score, jidx, n, width):
    sel = jnp.zeros(score.shape, F32)
    for _ in range(n):
        mx = jnp.max(score, axis=-1, keepdims=True)
        first = jnp.min(jnp.where(score == mx, jidx, width), axis=-1, keepdims=True)
        hit = jidx == first
        sel = jnp.where(hit, 1.0, sel)
        score = jnp.where(hit, -3.0 * BIG, score)
    return sel


def _moba_dec_kernel(pt_ref, q_ref, new_ref, bias_ref, ex_ref, cache_ref, o_ref, buf, sem, *, layer, n_pages, page):
    _fetch_pages(pt_ref, cache_ref, (buf,), (sem,), layer, n_pages, page)
    tk = n_pages * page
    nblk = tk // MOBA_BLOCK
    rows = lax.broadcasted_iota(jnp.int32, (ROWS, MIX_W), 0)
    lanes = lax.broadcasted_iota(jnp.int32, (ROWS, MIX_W), 1)
    hm = lanes // HEAD_DIM == rows
    qrows = jnp.where(hm, q_ref[0], 0.0).astype(BF16)
    kv = buf[...]
    kf = kv[:, 0:MIX_W]
    k = kf.astype(BF16)
    v = kv[:, MIX_W:2 * MIX_W].astype(BF16)
    kmean = jnp.mean(kf.reshape(nblk, MOBA_BLOCK, MIX_W), axis=1)
    kmean = jnp.concatenate([kmean, jnp.zeros((LANE - nblk, MIX_W), F32)], axis=0).astype(BF16)
    jidx = lax.broadcasted_iota(jnp.int32, (ROWS, LANE), 1)
    g = jnp.where(jidx < nblk, _dot_nt(qrows, kmean), -BIG)
    sel = jnp.zeros((ROWS, LANE), F32)
    for _ in range(min(MOBA_TOPK, nblk + 1)):
        mx = jnp.max(g, axis=-1, keepdims=True)
        first = jnp.min(jnp.where(g == mx, jidx, LANE), axis=-1, keepdims=True)
        hit = jidx == first
        sel = jnp.where(hit & (mx > -0.5 * BIG), 1.0, sel)
        g = jnp.where(hit, -3.0 * BIG, g)
    selk = _dot(sel.astype(BF16), ex_ref[...]) > 0.5
    scale = HEAD_DIM ** -0.5
    s = jnp.where(selk, _dot_nt(qrows, k) * scale + bias_ref[:, 0:tk], NEG)
    new = new_ref[0]
    s_new = _row_dot(qrows, new[:, 0:MIX_W]) * scale + bias_ref[:, tk:tk + 1]
    o = _softmax_av(s, s_new, v, new[:, MIX_W:2 * MIX_W])
    o_ref[0] = jnp.sum(jnp.where(hm, o, 0.0), axis=0, keepdims=True)


def _diff_dec_kernel(pt_ref, q_ref, new_ref, bias_ref, dl_ref, gain_ref, cache_ref, o_ref, buf, sem,
                     *, layer, n_pages, page, lam_init):
    _fetch_pages(pt_ref, cache_ref, (buf,), (sem,), layer, n_pages, page)
    tk = n_pages * page
    rows = lax.broadcasted_iota(jnp.int32, (ROWS, MIX_W), 0)
    lanes = lax.broadcasted_iota(jnp.int32, (ROWS, MIX_W), 1)
    qrows = jnp.where(lanes // DIFF_DH == rows, q_ref[0], 0.0).astype(BF16)
    kv = buf[...]
    k = kv[:, 0:MIX_W].astype(BF16)
    v = kv[:, MIX_W:2 * MIX_W].astype(BF16)
    scale = DIFF_DH ** -0.5
    s = _dot_nt(qrows, k) * scale + bias_ref[:, 0:tk]
    new = new_ref[0]
    s_new = _row_dot(qrows, new[:, 0:MIX_W]) * scale + bias_ref[:, tk:tk + 1]
    a = _softmax_av(s, s_new, v, new[:, MIX_W:2 * MIX_W])
    dl = dl_ref[...]
    lam = (jnp.exp(jnp.sum(dl[0:1] * dl[1:2], axis=-1, keepdims=True))
           - jnp.exp(jnp.sum(dl[2:3] * dl[3:4], axis=-1, keepdims=True)) + lam_init)
    sgn = jnp.where(rows % 2 == 0, 1.0, -lam)
    o = jnp.sum(jnp.where(lanes // HEAD_DIM == rows // 2, a * sgn, 0.0), axis=0, keepdims=True)
    for h in range(N_HEADS):
        oh = o[:, h * HEAD_DIM:(h + 1) * HEAD_DIM]
        o_ref[0, :, h * HEAD_DIM:(h + 1) * HEAD_DIM] = (
            oh * lax.rsqrt(jnp.mean(oh * oh, axis=-1, keepdims=True) + EPS) * gain_ref[...] * (1.0 - lam_init))


def _dsa_dec_kernel(pt_ref, q_ref, qi_ref, w_ref, new_ref, bias_ref, tri_ref, cache_ref, o_ref, buf, sem,
                    *, layer, n_pages, page):
    _fetch_pages(pt_ref, cache_ref, (buf,), (sem,), layer, n_pages, page)
    tk = n_pages * page
    kvi = buf[...]
    k = kvi[:, 0:HEAD_DIM].astype(BF16)
    v = kvi[:, HEAD_DIM:2 * HEAD_DIM].astype(BF16)
    ki = kvi[:, 2 * HEAD_DIM:DSA_ROW].astype(BF16)
    new = new_ref[0]
    qi = qi_ref[0].astype(BF16)
    w = w_ref[0]
    score = jnp.sum(jnp.maximum(_dot_nt(qi, ki), 0.0) * w, axis=0, keepdims=True)
    score_new = jnp.sum(jnp.maximum(_row_dot(qi, new[:, 2 * HEAD_DIM:DSA_ROW]), 0.0) * w, axis=0, keepdims=True)
    key = _sortable(score)
    key_new = _sortable(score_new)

    def count_ge(tau):
        return (jnp.sum(jnp.where(key >= tau, 1.0, 0.0), axis=-1, keepdims=True)
                + jnp.where(key_new >= tau, 1.0, 0.0))

    def bis_body(it, tau):
        cand = tau + lax.shift_left(jnp.int32(1), 31 - it)
        return jnp.where(count_ge(cand) >= float(DSA_TOPK), cand, tau)

    tau = lax.fori_loop(0, 32, bis_body, jnp.full((1, 1), INT_MIN, jnp.int32))
    need = float(DSA_TOPK) - count_ge(tau + 1)
    run = jnp.zeros((1, 1), F32)
    pieces = []
    for c in range(tk // LANE):
        kc = key[:, c * LANE:(c + 1) * LANE]
        eq = jnp.where(kc == tau, 1.0, 0.0)
        pref = run + _dot(jnp.broadcast_to(eq, (ROWS, LANE)).astype(BF16), tri_ref[...])[0:1]
        pieces.append(jnp.where((kc > tau) | ((kc == tau) & (pref <= need)), 1.0, 0.0))
        run = run + jnp.sum(eq, axis=-1, keepdims=True)
    mask = jnp.concatenate(pieces, axis=1) > 0.5
    sel_new = (key_new > tau) | ((key_new == tau) & (run + 1.0 <= need))
    q = q_ref[0].astype(BF16)
    scale = HEAD_DIM ** -0.5
    s = jnp.where(mask, _dot_nt(q, k) * scale + bias_ref[:, 0:tk], NEG)
    s_new = jnp.where(sel_new, _row_dot(q, new[:, 0:HEAD_DIM]) * scale + bias_ref[:, tk:tk + 1], NEG)
    o_ref[0] = _softmax_av(s, s_new, v, new[:, HEAD_DIM:2 * HEAD_DIM])


def _compress_math(c, i, pe_ref, w1_ref, w2_ref, g_ref):
    half = NSA_CMP_STRIDE * HEAD_DIM
    a = _dot((c + pe_ref[i, :, 0:half]).astype(BF16), w1_ref[i, 0:half, :])
    bm = _dot((c + pe_ref[i, :, half:2 * half]).astype(BF16), w1_ref[i, half:2 * half, :])
    hid = a + jnp.concatenate([bm[1:], jnp.zeros((1, bm.shape[1]), F32)], axis=0)
    y = _dot(jax.nn.gelu(hid).astype(BF16), w2_ref[i])
    if i == 0:
        y = y * lax.rsqrt(jnp.mean(y * y, axis=-1, keepdims=True) + EPS) * g_ref[...]
    return y


def _nsa_dec_kernel(pt_ref, q_ref, gl_ref, new_ref, win_ref, wnew_ref, pe_ref, w1_ref, w2_ref, kn_ref,
                    cb_ref, ov_ref, ex_ref, bias_ref, wb_ref, cache_ref, o_ref, buf, sem, buf_s, sem_s, c_sc,
                    *, layer, n_pages, page):
    _fetch_pages(pt_ref, cache_ref, (buf, buf_s), (sem, sem_s), layer, n_pages, page)
    tk = n_pages * page
    nch = tk // NSA_CMP_STRIDE
    n_win = win_ref.shape[2]
    scale = HEAD_DIM ** -0.5
    q = q_ref[0].astype(BF16)
    rowok = lax.broadcasted_iota(jnp.int32, (ROWS, 1), 0) < N_HEADS
    for l in range(NSA_CMP_STRIDE):
        rows = buf[pl.ds(l, nch, stride=NSA_CMP_STRIDE), :]
        c_sc[0, :, l * HEAD_DIM:(l + 1) * HEAD_DIM] = rows[:, 0:HEAD_DIM]
        c_sc[1, :, l * HEAD_DIM:(l + 1) * HEAD_DIM] = rows[:, HEAD_DIM:2 * HEAD_DIM]
    cmp = [_compress_math(c_sc[i], i, pe_ref, w1_ref, w2_ref, kn_ref).astype(BF16) for i in range(2)]
    cb = cb_ref[...]
    ok = cb > 0.5 * NEG
    z = jnp.where(ok, _dot_nt(q, cmp[0]) * scale + cb, NEG)
    z = z - jnp.max(z, axis=-1, keepdims=True)
    pc = jnp.where(ok & rowok, jnp.exp(z), 0.0)
    pc = (pc / jnp.maximum(jnp.sum(pc, axis=-1, keepdims=True), 1e-30)).astype(BF16)
    oc = _dot(pc, cmp[1])
    imp = jnp.sum(_dot(pc, ov_ref[...]), axis=0, keepdims=True)
    bt = tk // NSA_SLC_BLOCK
    jidx = lax.broadcasted_iota(jnp.int32, (1, LANE), 1)
    forced = (jidx == 0) | (jidx == bt) | (jidx == bt - 1)
    score = jnp.where(jidx <= bt, jnp.where(forced, BIG, imp), -BIG)
    sel = _top_lanes(score, jidx, min(NSA_N_SEL, bt + 1), LANE)
    sel_new = jnp.max(jnp.where(jidx == bt, sel, 0.0), axis=-1, keepdims=True) > 0.5
    selk = _dot(jnp.broadcast_to(sel, (ROWS, LANE)).astype(BF16), ex_ref[...]) > 0.5
    kv = buf_s[...]
    new = new_ref[0]
    s = jnp.where(selk, _dot_nt(q, kv[:, 0:HEAD_DIM].astype(BF16)) * scale + bias_ref[:, 0:tk], NEG)
    s_new = jnp.where(sel_new, _row_dot(q, new[:, 2 * HEAD_DIM:3 * HEAD_DIM]) * scale + bias_ref[:, tk:tk + 1], NEG)
    osel = _softmax_av(s, s_new, kv[:, HEAD_DIM:2 * HEAD_DIM].astype(BF16), new[:, 3 * HEAD_DIM:4 * HEAD_DIM])
    win = win_ref[0, 0]
    wnew = wnew_ref[0]
    sw = _dot_nt(q, win[:, 0:HEAD_DIM].astype(BF16)) * scale + wb_ref[:, 0:n_win]
    sw_new = _row_dot(q, wnew[:, 0:HEAD_DIM]) * scale + wb_ref[:, n_win:n_win + 1]
    ow = _softmax_av(sw, sw_new, win[:, HEAD_DIM:2 * HEAD_DIM].astype(BF16), wnew[:, HEAD_DIM:2 * HEAD_DIM])
    g = 1.0 / (1.0 + jnp.exp(-gl_ref[0]))
    o_ref[0] = g[:, 0:1] * oc + g[:, 1:2] * osel + g[:, 2:3] * ow


def _decode_call(body, page_table, cache, width, ins, in_specs, out_shape, out_spec, extra_scratch=()):
    b, n_pages = page_table.shape
    page = cache.shape[2]
    grid_spec = pltpu.PrefetchScalarGridSpec(
        num_scalar_prefetch=1, grid=(b,),
        in_specs=list(in_specs) + [pl.BlockSpec(memory_space=pl.ANY)],
        out_specs=out_spec,
        scratch_shapes=[pltpu.VMEM((n_pages * page, width), F32), pltpu.SemaphoreType.DMA((n_pages,))]
        + list(extra_scratch))
    return pl.pallas_call(body, grid_spec=grid_spec, out_shape=out_shape, compiler_params=_cparams(1))(
        page_table.reshape(-1), *ins, cache)


def _row_bias(tab_rows, pos, n_keys, kpos0=0, window=None):
    kpos = jnp.concatenate([kpos0 + jnp.arange(n_keys), jnp.full((1,), pos)])
    rel = pos - kpos
    bias = tab_rows[:, _t5_bucket(rel)]
    if window is not None:
        bias = jnp.where((rel < window)[None, :], bias, NEG)
    bias = jnp.pad(bias, ((0, ROWS - bias.shape[0]), (0, LANE - 1)))
    return bias.astype(F32)


def _pad_rows(a):
    return jnp.pad(a, ((0, 0), (0, ROWS - a.shape[1]), (0, 0)))


def _layer_sample(x, pos, caches, win_state, page_table, layer, lp, prep, rel_bias, lam_init):
    b = x.shape[0]
    n_pages = page_table.shape[1]
    page = caches[0].shape[2]
    tk = n_pages * page
    win_buf = win_state.shape[2]
    assert pos == tk and tk % MOBA_BLOCK == 0 and tk // MOBA_BLOCK <= LANE and tk // NSA_SLC_BLOCK < LANE
    assert tk // 4 >= DSA_TOPK and win_buf == NSA_WINDOW and tk // NSA_CMP_STRIDE == LANE
    w_main, g64, g32, w_gate, u_bf, vt_bf = prep
    (mq, nq, dq, sq, sqi, misc, moba_new, nsa_new, win_new, diff_new, dsa_new, _) = _in_proj(
        x.reshape(b, D_MODEL), lp['norm_mix'], w_main, g64, g32, b)
    depth, n_pool = caches[0].shape[0], caches[0].shape[1]
    c_moba = caches[0].reshape(depth, n_pool, page, 2 * MIX_W)
    c_nsa = caches[1].reshape(depth, n_pool, page, 4 * HEAD_DIM)
    c_diff = caches[2].reshape(depth, n_pool, page, 2 * MIX_W)
    c_dsa = caches[3]
    per_seq = lambda *shape: pl.BlockSpec((1,) + shape, lambda i, pt: (i,) + (0,) * len(shape))
    full = lambda a: pl.BlockSpec(a.shape, lambda i, pt: (0,) * a.ndim)
    tabs = [rel_bias[:, 4 * i:4 * i + 4].T for i in range(N_MIXERS)]
    kpos = jnp.arange(tk)
    kw = dict(layer=layer, n_pages=n_pages, page=page)

    bias = _row_bias(tabs[0], pos, tk)
    ex = (kpos[None, :] // MOBA_BLOCK == jnp.arange(LANE)[:, None]).astype(BF16)
    o_a = _decode_call(functools.partial(_moba_dec_kernel, **kw), page_table, c_moba, 2 * MIX_W,
                       (mq.reshape(b, 1, MIX_W), moba_new.reshape(b, 1, 2 * MIX_W), bias, ex),
                       [per_seq(1, MIX_W), per_seq(1, 2 * MIX_W), full(bias), full(ex)],
                       jax.ShapeDtypeStruct((b, 1, MIX_W), F32), per_seq(1, MIX_W))
    n_cmp = (tk + 1 - NSA_CMP_LEN) // NSA_CMP_STRIDE + 1
    cidx = jnp.arange(LANE)
    crel = pos - (cidx * NSA_CMP_STRIDE + NSA_CMP_LEN - 1)
    cbias = jnp.where(((crel >= 0) & (cidx < n_cmp))[None, :], tabs[1][:, _t5_bucket(crel)], NEG)
    cbias = jnp.pad(cbias, ((0, ROWS - N_HEADS), (0, 0))).astype(F32)
    cstart = cidx * NSA_CMP_STRIDE
    sstart = jnp.arange(LANE) * NSA_SLC_BLOCK
    overlap = ((cstart[:, None] < sstart[None, :] + NSA_SLC_BLOCK) & (cstart[:, None] + NSA_CMP_LEN > sstart[None, :])
               & (cidx < n_cmp)[:, None] & (sstart[None, :] <= tk)).astype(BF16)
    ex_s = (kpos[None, :] // NSA_SLC_BLOCK == jnp.arange(LANE)[:, None]).astype(BF16)
    bias_s = _row_bias(tabs[1], pos, tk)
    wbias = _row_bias(tabs[1], pos, win_buf, kpos0=pos - win_buf, window=NSA_WINDOW)
    glog = _pad_rows(jnp.pad(misc[:, MISC_G:MISC_G + 3 * N_HEADS].reshape(b, N_HEADS, 3), ((0, 0), (0, 0), (0, LANE - 3))))
    pe2 = lp['nsa_pe'].reshape(2, 1, NSA_CMP_LEN * HEAD_DIM)
    w1b, w2b = lp['nsa_w1'].astype(BF16), lp['nsa_w2'].astype(BF16)
    kn = lp['qk_gain'][3].reshape(1, HEAD_DIM)
    win4 = win_state.reshape(win_state.shape[0], b, win_buf, 2 * HEAD_DIM)
    q4 = lambda a: _pad_rows(a.reshape(b, N_HEADS, HEAD_DIM))
    o_b = _decode_call(functools.partial(_nsa_dec_kernel, **kw), page_table, c_nsa, 2 * HEAD_DIM,
                       (q4(nq), glog, nsa_new.reshape(b, 1, 4 * HEAD_DIM), win4, win_new.reshape(b, 1, 2 * HEAD_DIM),
                        pe2, w1b, w2b, kn, cbias, overlap, ex_s, bias_s, wbias),
                       [per_seq(ROWS, HEAD_DIM), per_seq(ROWS, LANE), per_seq(1, 4 * HEAD_DIM),
                        pl.BlockSpec((1, 1, win_buf, 2 * HEAD_DIM), lambda i, pt: (layer, i, 0, 0)),
                        per_seq(1, 2 * HEAD_DIM), full(pe2), full(w1b), full(w2b), full(kn), full(cbias),
                        full(overlap), full(ex_s), full(bias_s), full(wbias)],
                       jax.ShapeDtypeStruct((b, ROWS, HEAD_DIM), F32), per_seq(ROWS, HEAD_DIM),
                       extra_scratch=[pltpu.VMEM((tk, 2 * HEAD_DIM), F32), pltpu.SemaphoreType.DMA((n_pages,)),
                                      pltpu.VMEM((2, tk // NSA_CMP_STRIDE, NSA_CMP_STRIDE * HEAD_DIM), F32)])
    bias_d = _row_bias(jnp.repeat(tabs[2], 2, axis=0), pos, tk)
    dl = lp['diff_lambda'].astype(F32)
    gain = lp['diff_out_gain'].reshape(1, HEAD_DIM)
    o_c = _decode_call(functools.partial(_diff_dec_kernel, lam_init=lam_init, **kw), page_table, c_diff, 2 * MIX_W,
                       (dq.reshape(b, 1, MIX_W), diff_new.reshape(b, 1, 2 * MIX_W), bias_d, dl, gain),
                       [per_seq(1, MIX_W), per_seq(1, 2 * MIX_W), full(bias_d), full(dl), full(gain)],
                       jax.ShapeDtypeStruct((b, 1, MIX_W), F32), per_seq(1, MIX_W))
    bias_i = _row_bias(tabs[3], pos, tk)
    i = np.arange(LANE)
    tri = jnp.asarray(i[:, None] <= i[None, :], BF16)
    qi8 = _pad_rows(sqi.reshape(b, DSA_IDX_HEADS, DSA_IDX_DIM))
    w8 = _pad_rows(misc[:, MISC_W:MISC_W + DSA_IDX_HEADS].reshape(b, DSA_IDX_HEADS, 1))
    o_d = _decode_call(functools.partial(_dsa_dec_kernel, **kw), page_table, c_dsa, DSA_ROW,
                       (q4(sq), qi8, w8, dsa_new.reshape(b, 1, DSA_ROW), bias_i, tri),
                       [per_seq(ROWS, HEAD_DIM), per_seq(ROWS, DSA_IDX_DIM), per_seq(ROWS, 1), per_seq(1, DSA_ROW),
                        full(bias_i), full(tri)],
                       jax.ShapeDtypeStruct((b, ROWS, HEAD_DIM), F32), per_seq(ROWS, HEAD_DIM))
    heads = lambda o: o[:, :N_HEADS].reshape(b, MIX_W)
    x1, hf_bf, qh = _merge(x.reshape(b, D_MODEL), lp['norm_mix'], w_gate, o_a.reshape(b, MIX_W), heads(o_b),
                           o_c.reshape(b, MIX_W), heads(o_d), lp['w_branch'], lp['w_out'], lp['norm_ffn'],
                           lp['peer_wq'], b)
    x2 = _peer(hf_bf, qh, x1, lp['peer_subkeys'], u_bf, vt_bf, b)
    wrows = jnp.concatenate([win_state[layer][:, 1:], win_new.reshape(b, 1, 2, HEAD_DIM)], axis=1)
    return (x2.reshape(b, 1, D_MODEL), moba_new.reshape(b, 1, 2, N_HEADS, HEAD_DIM), nsa_new.reshape(b, 1, 4, HEAD_DIM),
            diff_new.reshape(b, 1, 2, N_HEADS, HEAD_DIM), dsa_new.reshape(b, 1, DSA_ROW), wrows)


def kernel(x_prompt, x_sample, cache_moba, cache_nsa, cache_diff, cache_dsa, state_nsa_win, page_table,
           rel_bias, norm_mix, w_in, qk_gain, nsa_pe, nsa_w1, nsa_w2, diff_qk_gain, diff_lambda,
           diff_out_gain, w_branch, w_out, norm_ffn, peer_wq, peer_subkeys, peer_u, peer_v):
    depth = w_in.shape[0]
    past_len = page_table.shape[1] * cache_moba.shape[2]
    win_buf = state_nsa_win.shape[2]
    assert x_sample.shape[1] == 1 and x_prompt.shape[1] >= win_buf and past_len >= win_buf
    caches = (cache_moba, cache_nsa, cache_diff, cache_dsa)
    yp, ys = x_prompt, x_sample
    rows_p, rows_s = [], []
    for l in range(depth):
        lp = {'norm_mix': norm_mix[l], 'qk_gain': qk_gain[l], 'nsa_pe': nsa_pe[l],
              'nsa_w1': nsa_w1[l], 'nsa_w2': nsa_w2[l], 'diff_lambda': diff_lambda[l],
              'diff_out_gain': diff_out_gain[l], 'w_branch': w_branch[l],
              'w_out': w_out[l], 'norm_ffn': norm_ffn[l], 'peer_wq': peer_wq[l], 'peer_subkeys': peer_subkeys[l]}
        prep = _prep_w_in(w_in[l], qk_gain[l], diff_qk_gain[l]) + (peer_u[l].astype(BF16), peer_v[l].T.astype(BF16))
        lam_init = 0.8 - 0.6 * math.exp(-0.3 * l)
        out_p = _layer_prompt(yp, lp, prep, rel_bias, lam_init)
        out_s = _layer_sample(ys, past_len, caches, state_nsa_win, page_table, l, lp, prep, rel_bias, lam_init)
        yp, ys = out_p[0], out_s[0]
        rows_p.append(out_p[1:5] + (out_p[5][:, -win_buf:],))
        rows_s.append(out_s[1:])

    def stk(rows, i):
        return jnp.stack([r[i] for r in rows], axis=0)

    return (yp, ys, stk(rows_p, 0), stk(rows_s, 0), stk(rows_p, 1), stk(rows_s, 1),
            stk(rows_p, 2), stk(rows_s, 2), stk(rows_p, 3), stk(rows_s, 3), stk(rows_p, 4), stk(rows_s, 4))
```

```python
import functools
import math

import jax
import jax.numpy as jnp
import numpy as np
from jax import lax
from jax.experimental import pallas as pl
from jax.experimental.pallas import tpu as pltpu

D_MODEL = 1024
N_MIXERS = 4
N_HEADS = 4
HEAD_DIM = 64
MIX_W = N_HEADS * HEAD_DIM
REL_BUCKETS = 32
REL_MAX_DIST = 128
MOBA_BLOCK = 256
MOBA_TOPK = 3
MOBA_QB = 32
NSA_CMP_LEN = 32
NSA_CMP_STRIDE = 16
NSA_CMP_HID = 128
NSA_SLC_BLOCK = 64
NSA_N_SEL = 16
NSA_WINDOW = 512
DIFF_DH = 32
DSA_IDX_HEADS = 4
DSA_IDX_DIM = 32
DSA_TOPK = 256
DSA_ROW = 2 * HEAD_DIM + DSA_IDX_DIM
PEER_HEADS = 8
PEER_KEYS = 128
PEER_TOPK = 16
PEER_QDIM = 256
PEER_CHUNK = 128
Q_BLOCK = 128
EPS = 1e-6
NEG = -1e30
BIG = 1e9

IN_SPLITS = (
    ('moba_q', MIX_W), ('moba_k', MIX_W), ('moba_v', MIX_W),
    ('nsa_q', MIX_W), ('nsa_kc', HEAD_DIM), ('nsa_vc', HEAD_DIM), ('nsa_ks', HEAD_DIM),
    ('nsa_vs', HEAD_DIM), ('nsa_kw', HEAD_DIM), ('nsa_vw', HEAD_DIM), ('nsa_g', 3 * N_HEADS),
    ('diff_q', N_HEADS * 2 * DIFF_DH), ('diff_k', N_HEADS * 2 * DIFF_DH), ('diff_v', N_HEADS * 2 * DIFF_DH),
    ('dsa_q', MIX_W), ('dsa_k', HEAD_DIM), ('dsa_v', HEAD_DIM),
    ('dsa_qi', DSA_IDX_HEADS * DSA_IDX_DIM), ('dsa_ki', DSA_IDX_DIM), ('dsa_w', DSA_IDX_HEADS),
    ('gates', N_MIXERS * D_MODEL),
)
IN_COLS = sum(s for _, s in IN_SPLITS)
IN_OFF = {}
_o = 0
for _n, _s in IN_SPLITS:
    IN_OFF[_n] = (_o, _s)
    _o += _s

LANE = 128
TILE = 256
VMEM_LIMIT = 56 * 1024 * 1024
F32 = jnp.float32
BF16 = jnp.bfloat16
INT_MIN = -2 ** 31

N64_W = 1280
N32_W = 512
RAW_W = 1152
MAIN_W = N64_W + N32_W + RAW_W
MISC_G = 0
MISC_W = 12


def _dot(a, b):
    return jnp.dot(a, b, preferred_element_type=F32)


def _dot_nt(a, b):
    return lax.dot_general(a, b, (((1,), (1,)), ((), ())), preferred_element_type=F32)


def _cparams(n_grid):
    return pltpu.CompilerParams(dimension_semantics=("arbitrary",) * n_grid, vmem_limit_bytes=VMEM_LIMIT)


def _t5_bucket(rel):
    n = jnp.maximum(rel, 0)
    exact = REL_BUCKETS // 2
    nf = jnp.maximum(n, 1).astype(jnp.float32)
    large = exact + (jnp.log(nf / exact) / math.log(REL_MAX_DIST / exact) * (REL_BUCKETS - exact)).astype(jnp.int32)
    return jnp.where(n < exact, n, jnp.minimum(large, REL_BUCKETS - 1))


def _bias_lookup(tab, rel):
    bucket = _t5_bucket(rel)[..., None]
    out = jnp.broadcast_to(tab[REL_BUCKETS - 1], bucket.shape[:-1] + (tab.shape[1],))
    for k in range(REL_BUCKETS - 1):
        out = jnp.where(bucket == k, tab[k], out)
    return out


def _bias_tiles(tab, tq, tk, n_d, window=None):
    r = jnp.arange(tq)[:, None]
    c = jnp.arange(tk)[None, :]
    rel = jnp.arange(n_d)[:, None, None] * tq + r - c
    ok = rel >= 0
    if window is not None:
        ok = ok & (rel < window)
    b = jnp.where(ok[..., None], _bias_lookup(tab, rel), NEG)
    return jnp.transpose(b, (3, 0, 1, 2)).astype(F32)


def _prep_w_in(w_in, qk_gain, diff_qk_gain):
    def col(name):
        o, s = IN_OFF[name]
        return w_in[:, o:o + s]

    z = lambda n: jnp.zeros((D_MODEL, n), w_in.dtype)
    w_main = jnp.concatenate([
        col('moba_q'), col('moba_k'), col('nsa_q'), col('dsa_q'), col('nsa_ks'), col('nsa_kw'), col('dsa_k'), z(64),
        col('diff_q'), col('diff_k'),
        col('moba_v'), col('diff_v'), col('nsa_kc'), col('nsa_vc'), col('nsa_vs'), col('nsa_vw'), col('dsa_v'),
        col('dsa_ki'), z(32), col('dsa_qi'), col('nsa_g'), col('dsa_w'), z(LANE - 16)], axis=1).astype(BF16)
    t4 = lambda g: jnp.tile(g, N_HEADS)
    g64 = jnp.concatenate([t4(qk_gain[0]), t4(qk_gain[1]), t4(qk_gain[2]), t4(qk_gain[6]),
                           qk_gain[4], qk_gain[5], qk_gain[7], jnp.zeros((64,), F32)]).reshape(1, N64_W)
    g32 = jnp.concatenate([jnp.tile(diff_qk_gain[0], 8), jnp.tile(diff_qk_gain[1], 8)]).reshape(1, N32_W)
    w_gate = col('gates').astype(BF16)
    return w_main, g64, g32, w_gate


def _block_diag_ones(seg):
    i = np.arange(LANE)
    return jnp.asarray((i[:, None] // seg) == (i[None, :] // seg), BF16)


def _inproj_kernel(x_ref, g_ref, w_ref, g64_ref, g32_ref, bd64_ref, bd32_ref,
                   mq_ref, nq_ref, dq_ref, sq_ref, sqi_ref, misc_ref,
                   moba_ref, nsa_ref, win_ref, diff_ref, dsa_ref, kmean_ref, vtm_ref, vtd_ref, vts_ref):
    x = x_ref[...]
    hn = (x * lax.rsqrt(jnp.mean(x * x, axis=-1, keepdims=True) + EPS) * g_ref[...]).astype(BF16)

    def segnorm(p, bd, seg, gain):
        x2 = p * p
        hi = x2.astype(BF16)
        lo = (x2 - hi.astype(F32)).astype(BF16)
        ss = _dot(hi, bd) + _dot(lo, bd)
        return p * lax.rsqrt(ss * (1.0 / seg) + EPS) * gain

    p1 = _dot(hn, w_ref[:, 0:N64_W])
    bd64 = bd64_ref[...]
    c = [segnorm(p1[:, i * LANE:(i + 1) * LANE], bd64, 64, g64_ref[:, i * LANE:(i + 1) * LANE])
         for i in range(N64_W // LANE)]
    mq_ref[:, 0:128] = c[0]
    mq_ref[:, 128:256] = c[1]
    moba_ref[:, 0:128] = c[2]
    moba_ref[:, 128:256] = c[3]
    kmean_ref[0] = jnp.concatenate([jnp.mean(c[2], axis=0, keepdims=True),
                                    jnp.mean(c[3], axis=0, keepdims=True)], axis=1)
    nq_ref[:, 0:128] = c[4]
    nq_ref[:, 128:256] = c[5]
    sq_ref[:, 0:128] = c[6]
    sq_ref[:, 128:256] = c[7]
    nsa_ref[:, 128:192] = c[8][:, 0:64]
    win_ref[:, 0:64] = c[8][:, 64:128]
    dsa_ref[:, 0:64] = c[9][:, 0:64]

    p2 = _dot(hn, w_ref[:, N64_W:N64_W + N32_W])
    bd32 = bd32_ref[...]
    d = [segnorm(p2[:, i * LANE:(i + 1) * LANE], bd32, 32, g32_ref[:, i * LANE:(i + 1) * LANE])
         for i in range(N32_W // LANE)]
    dq_ref[:, 0:128] = d[0]
    dq_ref[:, 128:256] = d[1]
    diff_ref[:, 0:128] = d[2]
    diff_ref[:, 128:256] = d[3]

    p3 = _dot(hn, w_ref[:, N64_W + N32_W:MAIN_W])
    moba_ref[:, 256:512] = p3[:, 0:256]
    diff_ref[:, 256:512] = p3[:, 256:512]
    nsa_ref[:, 0:128] = p3[:, 512:640]
    nsa_ref[:, 192:256] = p3[:, 640:704]
    win_ref[:, 64:128] = p3[:, 704:768]
    dsa_ref[:, 64:128] = p3[:, 768:832]
    dsa_ref[:, 128:160] = p3[:, 832:864]
    sqi_ref[...] = p3[:, 896:1024]
    misc_ref[...] = p3[:, 1024:1152]
    vtm_ref[0] = p3[:, 0:256].T.astype(BF16)
    vtd_ref[0] = p3[:, 256:512].T.astype(BF16)
    vts_ref[0] = p3[:, 640:896].T.astype(BF16)


def _in_proj(x2d, norm_g, w_main, g64, g32, tm):
    n = x2d.shape[0]
    row = lambda w: pl.BlockSpec((tm, w), lambda i: (i, 0))
    full = lambda a: pl.BlockSpec(a.shape, lambda i: (0,) * a.ndim)
    bd64, bd32 = _block_diag_ones(64), _block_diag_ones(32)
    g = norm_g.reshape(1, D_MODEL)
    widths = (MIX_W, MIX_W, MIX_W, MIX_W, 128, 128, 512, 256, 128, 512, DSA_ROW)
    out_shape = [jax.ShapeDtypeStruct((n, w), F32) for w in widths]
    out_shape.append(jax.ShapeDtypeStruct((n // tm, 1, MIX_W), F32))
    out_shape += [jax.ShapeDtypeStruct((n // tm, MIX_W, tm), BF16)] * 3
    out_specs = ([row(w) for w in widths] + [pl.BlockSpec((1, 1, MIX_W), lambda i: (i, 0, 0))]
                 + [pl.BlockSpec((1, MIX_W, tm), lambda i: (i, 0, 0))] * 3)
    return pl.pallas_call(
        _inproj_kernel,
        grid=(n // tm,),
        in_specs=[row(D_MODEL), full(g), full(w_main), full(g64), full(g32), full(bd64), full(bd32)],
        out_specs=out_specs,
        out_shape=out_shape,
        compiler_params=_cparams(1),
    )(x2d, g, w_main, g64, g32, bd64, bd32)


def _flash_init(m_sc, l_sc, acc_sc):
    m_sc[...] = jnp.full(m_sc.shape, NEG, F32)
    l_sc[...] = jnp.zeros(l_sc.shape, F32)
    acc_sc[...] = jnp.zeros(acc_sc.shape, F32)


def _flash_update_t(st, i, vt_bf, m_sc, l_sc, acc_sc):
    m_old = m_sc[i]
    m_new = jnp.maximum(m_old, jnp.max(st, axis=0, keepdims=True))
    alpha = jnp.exp(m_old - m_new)
    p = jnp.exp(st - m_new)
    l_sc[i] = alpha * l_sc[i] + jnp.sum(p, axis=0, keepdims=True)
    acc_sc[i] = alpha * acc_sc[i] + _dot(vt_bf, p.astype(BF16))
    m_sc[i] = m_new


def _flash_scratch_t(n_streams, tq, dv):
    return [pltpu.VMEM((n_streams, 1, tq), F32), pltpu.VMEM((n_streams, 1, tq), F32),
            pltpu.VMEM((n_streams, dv, tq), F32)]


def _bias_tiles_t(tab, tq, n_d, window=None):
    return jnp.swapaxes(_bias_tiles(tab, tq, tq, n_d, window), 2, 3)


def _transpose_kernel(x_ref, o_ref):
    o_ref[...] = x_ref[0].T.astype(BF16)


def _transpose_bf16(x, layer, te):
    _, e, d = x.shape
    return pl.pallas_call(
        _transpose_kernel,
        grid=(e // te,),
        in_specs=[pl.BlockSpec((1, te, d), lambda i: (layer, i, 0))],
        out_specs=pl.BlockSpec((d, te), lambda i: (0, i)),
        out_shape=jax.ShapeDtypeStruct((d, e), BF16),
        compiler_params=_cparams(1),
    )(x)


def _diff_kernel(q_ref, k_ref, v_ref, bt_ref, dl_ref, gain_ref, o_ref, m_sc, l_sc, acc_sc, *, lam_init):
    qi = pl.program_id(1)
    tq = q_ref.shape[1]
    scale = DIFF_DH ** -0.5
    q = q_ref[0].astype(BF16)
    _flash_init(m_sc, l_sc, acc_sc)

    def body(kj, carry):
        d = jnp.minimum(qi - kj, 2)
        off = pl.multiple_of(kj * tq, tq)
        k = k_ref[0, pl.ds(off, tq), :]
        vt = v_ref[0, kj]
        for h in range(N_HEADS):
            bias = bt_ref[h, d]
            vh = vt[h * HEAD_DIM:(h + 1) * HEAD_DIM, :]
            for m in range(2):
                lo = h * HEAD_DIM + m * DIFF_DH
                st = _dot_nt(k[:, lo:lo + DIFF_DH], q[:, lo:lo + DIFF_DH]) * scale + bias
                _flash_update_t(st, 2 * h + m, vh, m_sc, l_sc, acc_sc)
        return carry

    lax.fori_loop(0, qi + 1, body, 0)
    dl = dl_ref[...]
    lam = (jnp.exp(jnp.sum(dl[0:1] * dl[1:2], axis=-1, keepdims=True))
           - jnp.exp(jnp.sum(dl[2:3] * dl[3:4], axis=-1, keepdims=True)) + lam_init)
    outs = []
    for h in range(N_HEADS):
        o = acc_sc[2 * h] / l_sc[2 * h] - lam * (acc_sc[2 * h + 1] / l_sc[2 * h + 1])
        outs.append(o * lax.rsqrt(jnp.mean(o * o, axis=0, keepdims=True) + EPS) * gain_ref[...] * (1.0 - lam_init))
    o_ref[0] = jnp.concatenate(outs, axis=0).T


def _diff_prompt(dq, dk_bf, dv_t, tab, dl, out_gain, lam_init):
    b, t, _ = dq.shape
    tq = TILE
    bt = _bias_tiles_t(tab, tq, 3)
    gain = out_gain.reshape(HEAD_DIM, 1)
    full = lambda a: pl.BlockSpec(a.shape, lambda i, j: (0,) * a.ndim)
    return pl.pallas_call(
        functools.partial(_diff_kernel, lam_init=lam_init),
        grid=(b, t // tq),
        in_specs=[pl.BlockSpec((1, tq, MIX_W), lambda i, j: (i, j, 0)),
                  pl.BlockSpec((1, t, MIX_W), lambda i, j: (i, 0, 0)),
                  pl.BlockSpec((1, t // tq, MIX_W, tq), lambda i, j: (i, 0, 0, 0)),
                  full(bt), full(dl), full(gain)],
        out_specs=pl.BlockSpec((1, tq, MIX_W), lambda i, j: (i, j, 0)),
        out_shape=jax.ShapeDtypeStruct((b, t, MIX_W), F32),
        scratch_shapes=_flash_scratch_t(2 * N_HEADS, tq, HEAD_DIM),
        compiler_params=_cparams(2),
    )(dq, dk_bf, dv_t, bt, dl, gain)


def _moba_kernel(q_ref, k_ref, v_ref, km_ref, bt_ref, o_ref, sel_sc, m_sc, l_sc, acc_sc):
    qi = pl.program_id(1)
    tq = q_ref.shape[1]
    scale = HEAD_DIM ** -0.5
    q = q_ref[0].astype(BF16)
    km = km_ref[0].astype(BF16)
    jidx = lax.broadcasted_iota(jnp.int32, (LANE, tq), 0)
    _flash_init(m_sc, l_sc, acc_sc)
    for h in range(N_HEADS):
        hs = slice(h * HEAD_DIM, (h + 1) * HEAD_DIM)
        g = _dot_nt(km[:, hs], q[:, hs])
        g = jnp.where(jidx < qi, g, -BIG)
        sel = jidx == qi
        for _ in range(MOBA_TOPK):
            mx = jnp.max(g, axis=0, keepdims=True)
            first = jnp.min(jnp.where(g == mx, jidx, LANE), axis=0, keepdims=True)
            hit = jidx == first
            sel = sel | (hit & (mx > -0.5 * BIG))
            g = jnp.where(hit, -3.0 * BIG, g)
        sel_sc[h] = sel.astype(F32)

    def body(kj, carry):
        d = jnp.minimum(qi - kj, 2)
        off = pl.multiple_of(kj * tq, tq)
        k = k_ref[0, pl.ds(off, tq), :]
        vt = v_ref[0, kj]
        for h in range(N_HEADS):
            hs = slice(h * HEAD_DIM, (h + 1) * HEAD_DIM)
            colsel = sel_sc[h, pl.ds(kj, 1), :] > 0.5
            st = _dot_nt(k[:, hs], q[:, hs]) * scale + bt_ref[h, d]
            st = jnp.where(colsel, st, NEG)
            _flash_update_t(st, h, vt[hs, :], m_sc, l_sc, acc_sc)
        return carry

    lax.fori_loop(0, qi + 1, body, 0)
    o_ref[0] = jnp.concatenate([acc_sc[h] / l_sc[h] for h in range(N_HEADS)], axis=0).T


def _moba_prompt(mq, k_bf, v_t, kmean, tab):
    b, t, _ = mq.shape
    tq = TILE
    assert tq == MOBA_BLOCK and t % tq == 0 and t // tq <= LANE
    bt = _bias_tiles_t(tab, tq, 3)
    km = jnp.pad(kmean, ((0, 0), (0, LANE - kmean.shape[1]), (0, 0)))
    full = lambda a: pl.BlockSpec(a.shape, lambda i, j: (0,) * a.ndim)
    return pl.pallas_call(
        _moba_kernel,
        grid=(b, t // tq),
        in_specs=[pl.BlockSpec((1, tq, MIX_W), lambda i, j: (i, j, 0)),
                  pl.BlockSpec((1, t, MIX_W), lambda i, j: (i, 0, 0)),
                  pl.BlockSpec((1, t // tq, MIX_W, tq), lambda i, j: (i, 0, 0, 0)),
                  pl.BlockSpec((1, LANE, MIX_W), lambda i, j: (i, 0, 0)),
                  full(bt)],
        out_specs=pl.BlockSpec((1, tq, MIX_W), lambda i, j: (i, j, 0)),
        out_shape=jax.ShapeDtypeStruct((b, t, MIX_W), F32),
        scratch_shapes=[pltpu.VMEM((N_HEADS, LANE, tq), F32)] + _flash_scratch_t(N_HEADS, tq, HEAD_DIM),
        compiler_params=_cparams(2),
    )(mq, k_bf, v_t, km, bt)


def _sortable(x):
    x = jnp.where(x == 0.0, 0.0, x)
    bits = pltpu.bitcast(x, jnp.int32)
    return bits ^ ((bits >> 31) & 0x7FFFFFFF)


def _dsa_kernel(q_ref, qi_ref, misc_ref, k_ref, v_ref, ki_ref, bt_ref, tri_ref, o_ref,
                key_sc, m_sc, l_sc, acc_sc):
    qi = pl.program_id(1)
    tq = q_ref.shape[1]
    scale = HEAD_DIM ** -0.5
    q = q_ref[0].astype(BF16)
    qidx = qi_ref[0].astype(BF16)
    wt = misc_ref[0].T[MISC_W:MISC_W + DSA_IDX_HEADS, :]
    k_io = lax.broadcasted_iota(jnp.int32, (tq, tq), 0)
    q_io = lax.broadcasted_iota(jnp.int32, (tq, tq), 1)
    nk = qi + 1

    def score_body(kj, carry):
        off = pl.multiple_of(kj * tq, tq)
        ki = ki_ref[0, pl.ds(off, tq), :]
        sc = jnp.zeros((tq, tq), F32)
        for h in range(DSA_IDX_HEADS):
            idx = _dot_nt(ki, qidx[:, h * DSA_IDX_DIM:(h + 1) * DSA_IDX_DIM])
            sc = sc + jnp.maximum(idx, 0.0) * wt[h:h + 1, :]
        sc = jnp.where((kj < qi) | (k_io <= q_io), sc, -BIG)
        key_sc[kj] = _sortable(sc)
        return carry

    lax.fori_loop(0, nk, score_body, 0)

    def count_ge(tau):
        def cb(kj, cnt):
            return cnt + jnp.sum(jnp.where(key_sc[kj] >= tau, 1.0, 0.0), axis=0, keepdims=True)
        return lax.fori_loop(0, nk, cb, jnp.zeros((1, tq), F32))

    def bis_body(it, tau):
        cand = tau + lax.shift_left(jnp.int32(1), 31 - it)
        return jnp.where(count_ge(cand) >= float(DSA_TOPK), cand, tau)

    tau = lax.fori_loop(0, 32, bis_body, jnp.full((1, tq), INT_MIN, jnp.int32))
    n_gt = count_ge(tau + 1)
    need = float(DSA_TOPK) - n_gt
    tri = tri_ref[...]
    _flash_init(m_sc, l_sc, acc_sc)

    def att_body(kj, run):
        d = jnp.minimum(qi - kj, 2)
        off = pl.multiple_of(kj * tq, tq)
        key = key_sc[kj]
        eq = key == tau
        pref = run + _dot(tri, jnp.where(eq, 1.0, 0.0).astype(BF16))
        mask = (key > tau) | (eq & (pref <= need))
        k = k_ref[0, pl.ds(off, tq), :]
        vt = v_ref[0, kj]
        for h in range(N_HEADS):
            st = _dot_nt(k, q[:, h * HEAD_DIM:(h + 1) * HEAD_DIM]) * scale + bt_ref[h, d]
            st = jnp.where(mask, st, NEG)
            _flash_update_t(st, h, vt, m_sc, l_sc, acc_sc)
        return run + jnp.sum(jnp.where(eq, 1.0, 0.0), axis=0, keepdims=True)

    lax.fori_loop(0, nk, att_body, jnp.zeros((1, tq), F32))
    o_ref[0] = jnp.concatenate([acc_sc[h] / l_sc[h] for h in range(N_HEADS)], axis=0).T


def _dsa_prompt(sq, sqi, misc, k_bf, v_t, v_row, ki_bf, tab):
    b, t, _ = sq.shape
    tq = TILE
    assert tq >= DSA_TOPK and t // 4 >= DSA_TOPK
    bt = _bias_tiles_t(tab, tq, 3)
    i = np.arange(tq)
    tri = jnp.asarray(i[:, None] >= i[None, :], BF16)
    full = lambda a: pl.BlockSpec(a.shape, lambda i, j: (0,) * a.ndim)
    seq = lambda w: pl.BlockSpec((1, t, w), lambda i, j: (i, 0, 0))
    til = lambda w: pl.BlockSpec((1, tq, w), lambda i, j: (i, j, 0))
    return pl.pallas_call(
        _dsa_kernel,
        grid=(b, t // tq),
        in_specs=[til(MIX_W), til(128), til(128), seq(HEAD_DIM),
                  pl.BlockSpec((1, t // tq, HEAD_DIM, tq), lambda i, j: (i, 0, v_row, 0)),
                  seq(DSA_IDX_DIM), full(bt), full(tri)],
        out_specs=til(MIX_W),
        out_shape=jax.ShapeDtypeStruct((b, t, MIX_W), F32),
        scratch_shapes=[pltpu.VMEM((t // tq, tq, tq), jnp.int32)] + _flash_scratch_t(N_HEADS, tq, HEAD_DIM),
        compiler_params=_cparams(2),
    )(sq, sqi, misc, k_bf, v_t, ki_bf, bt, tri)


def _compress_kernel(c_ref, pe_ref, w1_ref, w2_ref, g_ref, o_ref):
    for i in range(2):
        o_ref[0, i] = _compress_math(c_ref[0, i], i, pe_ref, w1_ref, w2_ref, g_ref)


def _nsa_compress(kc, vc, pe, w1, w2, kn_gain):
    b, tc, _ = kc.shape
    nch = tc // NSA_CMP_STRIDE
    c = jnp.stack([kc, vc], axis=1).reshape(b, 2, nch, NSA_CMP_STRIDE * HEAD_DIM)
    pe2 = pe.reshape(2, 1, NSA_CMP_LEN * HEAD_DIM)
    full = lambda a: pl.BlockSpec(a.shape, lambda i: (0,) * a.ndim)
    g = kn_gain.reshape(1, HEAD_DIM)
    w1b, w2b = w1.astype(BF16), w2.astype(BF16)
    return pl.pallas_call(
        _compress_kernel,
        grid=(b,),
        in_specs=[pl.BlockSpec((1, 2, nch, NSA_CMP_STRIDE * HEAD_DIM), lambda i: (i, 0, 0, 0)),
                  full(pe2), full(w1b), full(w2b), full(g)],
        out_specs=pl.BlockSpec((1, 2, nch, HEAD_DIM), lambda i: (i, 0, 0, 0)),
        out_shape=jax.ShapeDtypeStruct((b, 2, nch, HEAD_DIM), F32),
        compiler_params=_cparams(1),
    )(c, pe2, w1b, w2b, g)


def _nsa_kernel(q_ref, misc_ref, kcmp_ref, vcmpt_ref, ks_ref, vs_ref, kw_ref, vw_ref, cb_ref, ov_ref, ex_ref,
                bt_ref, wbt_ref, o_ref, m_sc, l_sc, acc_sc, oc_sc):
    qi = pl.program_id(1)
    tq = q_ref.shape[1]
    n_slc = ov_ref.shape[0]
    scale = HEAD_DIM ** -0.5
    q = q_ref[0].astype(BF16)
    kcmp = kcmp_ref[0].astype(BF16)
    vcmpt = vcmpt_ref[0].astype(BF16)
    ovt = ov_ref[...]

    imp = jnp.zeros((n_slc, tq), F32)
    for h in range(N_HEADS):
        cb = cb_ref[h]
        ok = cb > 0.5 * NEG
        z = jnp.where(ok, _dot_nt(kcmp, q[:, h * HEAD_DIM:(h + 1) * HEAD_DIM]) * scale + cb, NEG)
        z = z - jnp.max(z, axis=0, keepdims=True)
        p = jnp.where(ok, jnp.exp(z), 0.0)
        p = p / jnp.maximum(jnp.sum(p, axis=0, keepdims=True), 1e-30)
        pb = p.astype(BF16)
        oc_sc[h] = _dot(vcmpt, pb)
        imp = imp + _dot(ovt, pb)

    jidx = lax.broadcasted_iota(jnp.int32, (n_slc, tq), 0)
    pos = qi * tq + lax.broadcasted_iota(jnp.int32, (n_slc, tq), 1)
    bt = pos // NSA_SLC_BLOCK
    forced = (jidx == 0) | (jidx == bt) | (jidx == bt - 1)
    score = jnp.where(jidx <= bt, jnp.where(forced, BIG, imp), -BIG)
    sel = jnp.zeros((n_slc, tq), F32)
    for _ in range(min(NSA_N_SEL, n_slc)):
        mx = jnp.max(score, axis=0, keepdims=True)
        first = jnp.min(jnp.where(score == mx, jidx, n_slc), axis=0, keepdims=True)
        hit = jidx == first
        sel = jnp.where(hit, 1.0, sel)
        score = jnp.where(hit, -3.0 * BIG, score)
    sel_bf = sel.astype(BF16)

    _flash_init(m_sc, l_sc, acc_sc)

    def slc_body(kj, carry):
        d = jnp.minimum(qi - kj, 2)
        off = pl.multiple_of(kj * tq, tq)
        mask = _dot(ex_ref[kj], sel_bf) > 0.5
        k = ks_ref[0, pl.ds(off, tq), :]
        vt = vs_ref[0, kj]
        for h in range(N_HEADS):
            st = _dot_nt(k, q[:, h * HEAD_DIM:(h + 1) * HEAD_DIM]) * scale + bt_ref[h, d]
            st = jnp.where(mask, st, NEG)
            _flash_update_t(st, h, vt, m_sc, l_sc, acc_sc)
        return carry

    lax.fori_loop(0, qi + 1, slc_body, 0)

    n_w = wbt_ref.shape[1]

    def win_body(kj, carry):
        d = qi - kj
        off = pl.multiple_of(kj * tq, tq)
        k = kw_ref[0, pl.ds(off, tq), :]
        vt = vw_ref[0, kj]
        for h in range(N_HEADS):
            st = _dot_nt(k, q[:, h * HEAD_DIM:(h + 1) * HEAD_DIM]) * scale + wbt_ref[h, d]
            _flash_update_t(st, N_HEADS + h, vt, m_sc, l_sc, acc_sc)
        return carry

    lax.fori_loop(jnp.maximum(qi - (n_w - 1), 0), qi + 1, win_body, 0)

    gl = misc_ref[0].T[MISC_G:MISC_G + 3 * N_HEADS, :]
    g = 1.0 / (1.0 + jnp.exp(-gl))
    outs = []
    for h in range(N_HEADS):
        outs.append(g[3 * h:3 * h + 1] * oc_sc[h]
                    + g[3 * h + 1:3 * h + 2] * (acc_sc[h] / l_sc[h])
                    + g[3 * h + 2:3 * h + 3] * (acc_sc[N_HEADS + h] / l_sc[N_HEADS + h]))
    o_ref[0] = jnp.concatenate(outs, axis=0).T


def _nsa_prompt(nq, misc, cmp_kv, ks_bf, kw_bf, v_t, vs_row, vw_row, tab):
    b, t, _ = nq.shape
    tq = TILE
    n_cmp = (t - NSA_CMP_LEN) // NSA_CMP_STRIDE + 1
    n_cpad = cmp_kv.shape[2]
    n_slc = t // NSA_SLC_BLOCK
    assert t % tq == 0 and tq % NSA_SLC_BLOCK == 0 and n_cpad >= n_cmp
    kcmp = cmp_kv[:, 0]
    vcmpt = jnp.swapaxes(cmp_kv[:, 1], 1, 2)
    cidx = jnp.arange(n_cpad)
    crel = jnp.arange(t)[None, :] - (cidx * NSA_CMP_STRIDE + NSA_CMP_LEN - 1)[:, None]
    cok = (crel >= 0) & (cidx < n_cmp)[:, None]
    cbias = jnp.transpose(jnp.where(cok[..., None], _bias_lookup(tab, crel), NEG), (2, 0, 1)).astype(F32)
    cstart = cidx * NSA_CMP_STRIDE
    sstart = jnp.arange(n_slc) * NSA_SLC_BLOCK
    overlap_t = ((cstart[None, :] < sstart[:, None] + NSA_SLC_BLOCK)
                 & (cstart[None, :] + NSA_CMP_LEN > sstart[:, None]) & (cidx < n_cmp)[None, :]).astype(BF16)
    kpos = jnp.arange(t).reshape(t // tq, tq, 1)
    expand_t = (kpos // NSA_SLC_BLOCK == jnp.arange(n_slc)[None, None, :]).astype(BF16)
    bt = _bias_tiles_t(tab, tq, 3)
    n_w = (NSA_WINDOW - 1 + tq - 1) // tq + 1
    wbt = _bias_tiles_t(tab, tq, n_w, window=NSA_WINDOW)
    full = lambda a: pl.BlockSpec(a.shape, lambda i, j: (0,) * a.ndim)
    seq = lambda w: pl.BlockSpec((1, t, w), lambda i, j: (i, 0, 0))
    seq_t = lambda r: pl.BlockSpec((1, t // tq, HEAD_DIM, tq), lambda i, j: (i, 0, r, 0))
    til = lambda w: pl.BlockSpec((1, tq, w), lambda i, j: (i, j, 0))
    return pl.pallas_call(
        _nsa_kernel,
        grid=(b, t // tq),
        in_specs=[til(MIX_W), til(128),
                  pl.BlockSpec((1, n_cpad, HEAD_DIM), lambda i, j: (i, 0, 0)),
                  pl.BlockSpec((1, HEAD_DIM, n_cpad), lambda i, j: (i, 0, 0)),
                  seq(HEAD_DIM), seq_t(vs_row), seq(HEAD_DIM), seq_t(vw_row),
                  pl.BlockSpec((N_HEADS, n_cpad, tq), lambda i, j: (0, 0, j)),
                  full(overlap_t), full(expand_t), full(bt), full(wbt)],
        out_specs=til(MIX_W),
        out_shape=jax.ShapeDtypeStruct((b, t, MIX_W), F32),
        scratch_shapes=_flash_scratch_t(2 * N_HEADS, tq, HEAD_DIM) + [pltpu.VMEM((N_HEADS, HEAD_DIM, tq), F32)],
        compiler_params=_cparams(2),
    )(nq, misc, kcmp, vcmpt, ks_bf, v_t, kw_bf, v_t, cbias, overlap_t, expand_t, bt, wbt)


def _merge_kernel(x_ref, g_ref, wg_ref, oa_ref, ob_ref, oc_ref, od_ref, wb_ref, wo_ref, gf_ref, wq_ref,
                  x1_ref, hf_ref, qh_ref):
    x = x_ref[...]
    hn = (x * lax.rsqrt(jnp.mean(x * x, axis=-1, keepdims=True) + EPS) * g_ref[...]).astype(BF16)
    merged = jnp.zeros(x.shape, F32)
    for i, o_ref in enumerate((oa_ref, ob_ref, oc_ref, od_ref)):
        gl = _dot(hn, wg_ref[:, i * D_MODEL:(i + 1) * D_MODEL])
        z = _dot(o_ref[...].astype(BF16), wb_ref[i])
        merged = merged + z / (1.0 + jnp.exp(-gl))
    x1 = x + _dot(merged.astype(BF16), wo_ref[...])
    x1_ref[...] = x1
    hf = (x1 * lax.rsqrt(jnp.mean(x1 * x1, axis=-1, keepdims=True) + EPS) * gf_ref[...])
    hfb = hf.astype(BF16)
    hf_ref[...] = hfb
    qh_ref[...] = _dot(hfb, wq_ref[...])


def _merge(x2d, norm_g, w_gate, o_a, o_b, o_c, o_d, w_branch, w_out, norm_ffn, wq, tm):
    n = x2d.shape[0]
    row = lambda w: pl.BlockSpec((tm, w), lambda i: (i, 0))
    full = lambda a: pl.BlockSpec(a.shape, lambda i: (0,) * a.ndim, pipeline_mode=pl.Buffered(1))
    g = norm_g.reshape(1, D_MODEL)
    gf = norm_ffn.reshape(1, D_MODEL)
    wb = w_branch.astype(BF16)
    wo = w_out.astype(BF16)
    wqb = wq.astype(BF16)
    nq = wq.shape[1]
    return pl.pallas_call(
        _merge_kernel,
        grid=(n // tm,),
        in_specs=[row(D_MODEL), full(g), full(w_gate), row(MIX_W), row(MIX_W), row(MIX_W), row(MIX_W),
                  full(wb), full(wo), full(gf), full(wqb)],
        out_specs=[row(D_MODEL), row(D_MODEL), row(nq)],
        out_shape=[jax.ShapeDtypeStruct((n, D_MODEL), F32), jax.ShapeDtypeStruct((n, D_MODEL), BF16),
                   jax.ShapeDtypeStruct((n, nq), F32)],
        compiler_params=_cparams(1),
    )(x2d, g, w_gate, o_a, o_b, o_c, o_d, wb, wo, gf, wqb)


PEER_I1_CHUNK = 8
PEER_ECHUNK = PEER_I1_CHUNK * PEER_KEYS


def _top_desc(x, n, iota0):
    rows = []
    big = x.shape[0]
    for _ in range(n):
        mx = jnp.max(x, axis=0, keepdims=True)
        first = jnp.min(jnp.where(x == mx, iota0, big), axis=0, keepdims=True)
        x = jnp.where(iota0 == first, -jnp.inf, x)
        rows.append(mx)
    return rows


def _peer_kernel(hf_ref, qh_ref, sk_ref, u_ref, vt_ref, x1_ref, o_ref, s1_sc, s2_sc, e1_sc, e2_sc, tau_sc, acc_sc):
    c = pl.program_id(1)
    tm = hf_ref.shape[0]
    half = PEER_QDIM // 2

    @pl.when(c == 0)
    def _():
        acc_sc[...] = jnp.zeros(acc_sc.shape, F32)
        io_k = lax.broadcasted_iota(jnp.int32, (PEER_KEYS, tm), 0)
        n_cand = sum(PEER_TOPK // (a + 1) for a in range(PEER_TOPK))
        n_cand_pad = -n_cand % 8
        io_c = lax.broadcasted_iota(jnp.int32, (n_cand + n_cand_pad, tm), 0)
        for h in range(PEER_HEADS):
            qh = qh_ref[:, h * PEER_QDIM:(h + 1) * PEER_QDIM].astype(BF16)
            s1 = _dot_nt(sk_ref[0, h], qh[:, 0:half])
            s2 = _dot_nt(sk_ref[1, h], qh[:, half:PEER_QDIM])
            t1 = _top_desc(s1, PEER_TOPK, io_k)
            t2 = jnp.concatenate(_top_desc(s2, PEER_TOPK, io_k), axis=0)
            cand = jnp.concatenate([t1[a] + t2[0:PEER_TOPK // (a + 1)] for a in range(PEER_TOPK)]
                                   + [jnp.full((n_cand_pad, tm), -jnp.inf, F32)], axis=0)
            ts = _top_desc(cand, PEER_TOPK, io_c)
            zsum = ts[0] * 0.0
            for a in range(PEER_TOPK):
                zsum = zsum + jnp.exp(ts[a] - ts[0])
            m1 = t1[0]
            m2 = t2[0:1]
            s1_sc[h] = s1
            s2_sc[h] = s2
            e1_sc[h] = jnp.exp(s1 - m1) / zsum
            e2_sc[h] = jnp.exp(s2 - m2)
            tau_sc[h] = jnp.broadcast_to(ts[PEER_TOPK - 1], (8, tm))

    act = jax.nn.gelu(_dot_nt(u_ref[...], hf_ref[...]))
    for j in range(PEER_I1_CHUNK):
        i1 = c * PEER_I1_CHUNK + j
        wj = jnp.zeros((PEER_KEYS, tm), F32)
        for h in range(PEER_HEADS):
            tot = s1_sc[h, pl.ds(i1, 1), :] + s2_sc[h]
            wj = wj + jnp.where(tot >= tau_sc[h, 0:1, :], e1_sc[h, pl.ds(i1, 1), :] * e2_sc[h], 0.0)
        gj = (wj * act[j * PEER_KEYS:(j + 1) * PEER_KEYS, :]).astype(BF16)
        acc_sc[...] += _dot(vt_ref[:, j * PEER_KEYS:(j + 1) * PEER_KEYS], gj)

    @pl.when(c == pl.num_programs(1) - 1)
    def _():
        o_ref[...] = x1_ref[...] + acc_sc[...].T


def _peer(hf_bf, qh, x1, subkeys, u_bf, vt_bf, tm):
    n = hf_bf.shape[0]
    n_exp = u_bf.shape[0]
    sk = subkeys.astype(BF16)
    nc = n_exp // PEER_ECHUNK
    return pl.pallas_call(
        _peer_kernel,
        grid=(n // tm, nc),
        in_specs=[pl.BlockSpec((tm, D_MODEL), lambda i, c: (i, 0)),
                  pl.BlockSpec((tm, PEER_HEADS * PEER_QDIM), lambda i, c: (i, 0)),
                  pl.BlockSpec(sk.shape, lambda i, c: (0, 0, 0, 0)),
                  pl.BlockSpec((PEER_ECHUNK, D_MODEL), lambda i, c: (c, 0)),
                  pl.BlockSpec((D_MODEL, PEER_ECHUNK), lambda i, c: (0, c)),
                  pl.BlockSpec((tm, D_MODEL), lambda i, c: (i, 0))],
        out_specs=pl.BlockSpec((tm, D_MODEL), lambda i, c: (i, 0)),
        out_shape=jax.ShapeDtypeStruct((n, D_MODEL), F32),
        scratch_shapes=[pltpu.VMEM((PEER_HEADS, PEER_KEYS, tm), F32)] * 4
        + [pltpu.VMEM((PEER_HEADS, 8, tm), F32), pltpu.VMEM((D_MODEL, tm), F32)],
        compiler_params=_cparams(2),
    )(hf_bf, qh, sk, u_bf, vt_bf, x1)


def _layer_prompt(x, lp, prep, rel_bias, lam_init):
    b, t, _ = x.shape
    n = b * t
    w_main, g64, g32, w_gate, u_bf, vt_bf = prep
    (mq, nq, dq, sq, sqi, misc, moba_new, nsa_new, win_new, diff_new, dsa_new, kmean, vt_moba, vt_diff, vt_s) = _in_proj(
        x.reshape(n, D_MODEL), lp['norm_mix'], w_main, g64, g32, TILE)
    r3 = lambda a: a.reshape(b, t, a.shape[-1])
    moba3, nsa3, win3, diff3, dsa3 = r3(moba_new), r3(nsa_new), r3(win_new), r3(diff_new), r3(dsa_new)
    bf = lambda a: a.astype(BF16)
    vt = lambda a: a.reshape(b, t // TILE, MIX_W, TILE)
    o_a = _moba_prompt(r3(mq), bf(moba3[..., :MIX_W]), vt(vt_moba),
                       kmean.reshape(b, t // TILE, MIX_W), rel_bias[:, 0:4])
    t16 = t // NSA_CMP_STRIDE * NSA_CMP_STRIDE
    cmp_kv = _nsa_compress(nsa3[:, :t16, 0:64], nsa3[:, :t16, 64:128], lp['nsa_pe'], lp['nsa_w1'], lp['nsa_w2'],
                           lp['qk_gain'][3])
    o_b = _nsa_prompt(r3(nq), r3(misc), cmp_kv, bf(nsa3[..., 128:192]), bf(win3[..., 0:64]), vt(vt_s), 0, 1,
                      rel_bias[:, 4:8])
    o_c = _diff_prompt(r3(dq), bf(diff3[..., :MIX_W]), vt(vt_diff), rel_bias[:, 8:12],
                       lp['diff_lambda'].astype(F32), lp['diff_out_gain'], lam_init)
    o_d = _dsa_prompt(r3(sq), r3(sqi), r3(misc), bf(dsa3[..., 0:64]), vt(vt_s), 2, bf(dsa3[..., 128:160]),
                      rel_bias[:, 12:16])
    f2 = lambda a: a.reshape(n, MIX_W)
    x1, hf_bf, qh = _merge(x.reshape(n, D_MODEL), lp['norm_mix'], w_gate, f2(o_a), f2(o_b), f2(o_c), f2(o_d),
                           lp['w_branch'], lp['w_out'], lp['norm_ffn'], lp['peer_wq'], TILE)
    x2 = _peer(hf_bf, qh, x1, lp['peer_subkeys'], u_bf, vt_bf, 512)
    return (x2.reshape(b, t, D_MODEL), moba3.reshape(b, t, 2, N_HEADS, HEAD_DIM), nsa3.reshape(b, t, 4, HEAD_DIM),
            diff3.reshape(b, t, 2, N_HEADS, HEAD_DIM), dsa3, win3.reshape(b, t, 2, HEAD_DIM))


ROWS = 8


def _fetch_pages(pt_ref, cache_ref, bufs, sems, layer, n_pages, page):
    b = pl.program_id(0)
    copies = []
    for j in range(n_pages):
        src = cache_ref.at[layer, pt_ref[b * n_pages + j]]
        for i, (buf, sem) in enumerate(zip(bufs, sems)):
            w = buf.shape[1]
            part = src if len(bufs) == 1 else src.at[:, pl.ds(i * w, w)]
            cp = pltpu.make_async_copy(part, buf.at[pl.ds(j * page, page), :], sem.at[j])
            cp.start()
            copies.append(cp)
    for cp in copies:
        cp.wait()


def _row_dot(qrows_bf, new_row):
    return jnp.sum(qrows_bf.astype(F32) * new_row.astype(BF16).astype(F32), axis=-1, keepdims=True)


def _softmax_av(s, s_new, v_bf, v_new):
    m = jnp.maximum(jnp.max(s, axis=-1, keepdims=True), s_new)
    p = jnp.exp(s - m)
    p_new = jnp.exp(s_new - m)
    den = jnp.sum(p, axis=-1, keepdims=True) + p_new
    num = _dot(p.astype(BF16), v_bf) + p_new.astype(BF16).astype(F32) * v_new.astype(BF16).astype(F32)
    return num / jnp.maximum(den, 1e-30)


def _top_lanes(score, jidx, n, width):
    sel = jnp.zeros(score.shape, F32)
    for _ in range(n):
        mx = jnp.max(score, axis=-1, keepdims=True)
        first = jnp.min(jnp.where(score == mx, jidx, width), axis=-1, keepdims=True)
        hit = jidx == first
        sel = jnp.where(hit, 1.0, sel)
        score = jnp.where(hit, -3.0 * BIG, score)
    return sel


def _moba_dec_kernel(pt_ref, q_ref, new_ref, bias_ref, ex_ref, cache_ref, o_ref, buf, sem, *, layer, n_pages, page):
    _fetch_pages(pt_ref, cache_ref, (buf,), (sem,), layer, n_pages, page)
    tk = n_pages * page
    nblk = tk // MOBA_BLOCK
    rows = lax.broadcasted_iota(jnp.int32, (ROWS, MIX_W), 0)
    lanes = lax.broadcasted_iota(jnp.int32, (ROWS, MIX_W), 1)
    hm = lanes // HEAD_DIM == rows
    qrows = jnp.where(hm, q_ref[0], 0.0).astype(BF16)
    kv = buf[...]
    kf = kv[:, 0:MIX_W]
    k = kf.astype(BF16)
    v = kv[:, MIX_W:2 * MIX_W].astype(BF16)
    kmean = jnp.mean(kf.reshape(nblk, MOBA_BLOCK, MIX_W), axis=1)
    kmean = jnp.concatenate([kmean, jnp.zeros((LANE - nblk, MIX_W), F32)], axis=0).astype(BF16)
    jidx = lax.broadcasted_iota(jnp.int32, (ROWS, LANE), 1)
    g = jnp.where(jidx < nblk, _dot_nt(qrows, kmean), -BIG)
    sel = jnp.zeros((ROWS, LANE), F32)
    for _ in range(min(MOBA_TOPK, nblk + 1)):
        mx = jnp.max(g, axis=-1, keepdims=True)
        first = jnp.min(jnp.where(g == mx, jidx, LANE), axis=-1, keepdims=True)
        hit = jidx == first
        sel = jnp.where(hit & (mx > -0.5 * BIG), 1.0, sel)
        g = jnp.where(hit, -3.0 * BIG, g)
    selk = _dot(sel.astype(BF16), ex_ref[...]) > 0.5
    scale = HEAD_DIM ** -0.5
    s = jnp.where(selk, _dot_nt(qrows, k) * scale + bias_ref[:, 0:tk], NEG)
    new = new_ref[0]
    s_new = _row_dot(qrows, new[:, 0:MIX_W]) * scale + bias_ref[:, tk:tk + 1]
    o = _softmax_av(s, s_new, v, new[:, MIX_W:2 * MIX_W])
    o_ref[0] = jnp.sum(jnp.where(hm, o, 0.0), axis=0, keepdims=True)


def _diff_dec_kernel(pt_ref, q_ref, new_ref, bias_ref, dl_ref, gain_ref, cache_ref, o_ref, buf, sem,
                     *, layer, n_pages, page, lam_init):
    _fetch_pages(pt_ref, cache_ref, (buf,), (sem,), layer, n_pages, page)
    tk = n_pages * page
    rows = lax.broadcasted_iota(jnp.int32, (ROWS, MIX_W), 0)
    lanes = lax.broadcasted_iota(jnp.int32, (ROWS, MIX_W), 1)
    qrows = jnp.where(lanes // DIFF_DH == rows, q_ref[0], 0.0).astype(BF16)
    kv = buf[...]
    k = kv[:, 0:MIX_W].astype(BF16)
    v = kv[:, MIX_W:2 * MIX_W].astype(BF16)
    scale = DIFF_DH ** -0.5
    s = _dot_nt(qrows, k) * scale + bias_ref[:, 0:tk]
    new = new_ref[0]
    s_new = _row_dot(qrows, new[:, 0:MIX_W]) * scale + bias_ref[:, tk:tk + 1]
    a = _softmax_av(s, s_new, v, new[:, MIX_W:2 * MIX_W])
    dl = dl_ref[...]
    lam = (jnp.exp(jnp.sum(dl[0:1] * dl[1:2], axis=-1, keepdims=True))
           - jnp.exp(jnp.sum(dl[2:3] * dl[3:4], axis=-1, keepdims=True)) + lam_init)
    sgn = jnp.where(rows % 2 == 0, 1.0, -lam)
    o = jnp.sum(jnp.where(lanes // HEAD_DIM == rows // 2, a * sgn, 0.0), axis=0, keepdims=True)
    for h in range(N_HEADS):
        oh = o[:, h * HEAD_DIM:(h + 1) * HEAD_DIM]
        o_ref[0, :, h * HEAD_DIM:(h + 1) * HEAD_DIM] = (
            oh * lax.rsqrt(jnp.mean(oh * oh, axis=-1, keepdims=True) + EPS) * gain_ref[...] * (1.0 - lam_init))


def _dsa_dec_kernel(pt_ref, q_ref, qi_ref, w_ref, new_ref, bias_ref, tri_ref, cache_ref, o_ref, buf, sem,
                    *, layer, n_pages, page):
    _fetch_pages(pt_ref, cache_ref, (buf,), (sem,), layer, n_pages, page)
    tk = n_pages * page
    kvi = buf[...]
    k = kvi[:, 0:HEAD_DIM].astype(BF16)
    v = kvi[:, HEAD_DIM:2 * HEAD_DIM].astype(BF16)
    ki = kvi[:, 2 * HEAD_DIM:DSA_ROW].astype(BF16)
    new = new_ref[0]
    qi = qi_ref[0].astype(BF16)
    w = w_ref[0]
    score = jnp.sum(jnp.maximum(_dot_nt(qi, ki), 0.0) * w, axis=0, keepdims=True)
    score_new = jnp.sum(jnp.maximum(_row_dot(qi, new[:, 2 * HEAD_DIM:DSA_ROW]), 0.0) * w, axis=0, keepdims=True)
    key = _sortable(score)
    key_new = _sortable(score_new)

    def count_ge(tau):
        return (jnp.sum(jnp.where(key >= tau, 1.0, 0.0), axis=-1, keepdims=True)
                + jnp.where(key_new >= tau, 1.0, 0.0))

    def bis_body(it, tau):
        cand = tau + lax.shift_left(jnp.int32(1), 31 - it)
        return jnp.where(count_ge(cand) >= float(DSA_TOPK), cand, tau)

    tau = lax.fori_loop(0, 32, bis_body, jnp.full((1, 1), INT_MIN, jnp.int32))
    need = float(DSA_TOPK) - count_ge(tau + 1)
    run = jnp.zeros((1, 1), F32)
    pieces = []
    for c in range(tk // LANE):
        kc = key[:, c * LANE:(c + 1) * LANE]
        eq = jnp.where(kc == tau, 1.0, 0.0)
        pref = run + _dot(jnp.broadcast_to(eq, (ROWS, LANE)).astype(BF16), tri_ref[...])[0:1]
        pieces.append(jnp.where((kc > tau) | ((kc == tau) & (pref <= need)), 1.0, 0.0))
        run = run + jnp.sum(eq, axis=-1, keepdims=True)
    mask = jnp.concatenate(pieces, axis=1) > 0.5
    sel_new = (key_new > tau) | ((key_new == tau) & (run + 1.0 <= need))
    q = q_ref[0].astype(BF16)
    scale = HEAD_DIM ** -0.5
    s = jnp.where(mask, _dot_nt(q, k) * scale + bias_ref[:, 0:tk], NEG)
    s_new = jnp.where(sel_new, _row_dot(q, new[:, 0:HEAD_DIM]) * scale + bias_ref[:, tk:tk + 1], NEG)
    o_ref[0] = _softmax_av(s, s_new, v, new[:, HEAD_DIM:2 * HEAD_DIM])


def _compress_math(c, i, pe_ref, w1_ref, w2_ref, g_ref):
    half = NSA_CMP_STRIDE * HEAD_DIM
    a = _dot((c + pe_ref[i, :, 0:half]).astype(BF16), w1_ref[i, 0:half, :])
    bm = _dot((c + pe_ref[i, :, half:2 * half]).astype(BF16), w1_ref[i, half:2 * half, :])
    hid = a + jnp.concatenate([bm[1:], jnp.zeros((1, bm.shape[1]), F32)], axis=0)
    y = _dot(jax.nn.gelu(hid).astype(BF16), w2_ref[i])
    if i == 0:
        y = y * lax.rsqrt(jnp.mean(y * y, axis=-1, keepdims=True) + EPS) * g_ref[...]
    return y


def _nsa_dec_kernel(pt_ref, q_ref, gl_ref, new_ref, win_ref, wnew_ref, pe_ref, w1_ref, w2_ref, kn_ref,
                    cb_ref, ov_ref, ex_ref, bias_ref, wb_ref, cache_ref, o_ref, buf, sem, buf_s, sem_s, c_sc,
                    *, layer, n_pages, page):
    _fetch_pages(pt_ref, cache_ref, (buf, buf_s), (sem, sem_s), layer, n_pages, page)
    tk = n_pages * page
    nch = tk // NSA_CMP_STRIDE
    n_win = win_ref.shape[2]
    scale = HEAD_DIM ** -0.5
    q = q_ref[0].astype(BF16)
    rowok = lax.broadcasted_iota(jnp.int32, (ROWS, 1), 0) < N_HEADS
    for l in range(NSA_CMP_STRIDE):
        rows = buf[pl.ds(l, nch, stride=NSA_CMP_STRIDE), :]
        c_sc[0, :, l * HEAD_DIM:(l + 1) * HEAD_DIM] = rows[:, 0:HEAD_DIM]
        c_sc[1, :, l * HEAD_DIM:(l + 1) * HEAD_DIM] = rows[:, HEAD_DIM:2 * HEAD_DIM]
    cmp = [_compress_math(c_sc[i], i, pe_ref, w1_ref, w2_ref, kn_ref).astype(BF16) for i in range(2)]
    cb = cb_ref[...]
    ok = cb > 0.5 * NEG
    z = jnp.where(ok, _dot_nt(q, cmp[0]) * scale + cb, NEG)
    z = z - jnp.max(z, axis=-1, keepdims=True)
    pc = jnp.where(ok & rowok, jnp.exp(z), 0.0)
    pc = (pc / jnp.maximum(jnp.sum(pc, axis=-1, keepdims=True), 1e-30)).astype(BF16)
    oc = _dot(pc, cmp[1])
    imp = jnp.sum(_dot(pc, ov_ref[...]), axis=0, keepdims=True)
    bt = tk // NSA_SLC_BLOCK
    jidx = lax.broadcasted_iota(jnp.int32, (1, LANE), 1)
    forced = (jidx == 0) | (jidx == bt) | (jidx == bt - 1)
    score = jnp.where(jidx <= bt, jnp.where(forced, BIG, imp), -BIG)
    sel = _top_lanes(score, jidx, min(NSA_N_SEL, bt + 1), LANE)
    sel_new = jnp.max(jnp.where(jidx == bt, sel, 0.0), axis=-1, keepdims=True) > 0.5
    selk = _dot(jnp.broadcast_to(sel, (ROWS, LANE)).astype(BF16), ex_ref[...]) > 0.5
    kv = buf_s[...]
    new = new_ref[0]
    s = jnp.where(selk, _dot_nt(q, kv[:, 0:HEAD_DIM].astype(BF16)) * scale + bias_ref[:, 0:tk], NEG)
    s_new = jnp.where(sel_new, _row_dot(q, new[:, 2 * HEAD_DIM:3 * HEAD_DIM]) * scale + bias_ref[:, tk:tk + 1], NEG)
    osel = _softmax_av(s, s_new, kv[:, HEAD_DIM:2 * HEAD_DIM].astype(BF16), new[:, 3 * HEAD_DIM:4 * HEAD_DIM])
    win = win_ref[0, 0]
    wnew = wnew_ref[0]
    sw = _dot_nt(q, win[:, 0:HEAD_DIM].astype(BF16)) * scale + wb_ref[:, 0:n_win]
    sw_new = _row_dot(q, wnew[:, 0:HEAD_DIM]) * scale + wb_ref[:, n_win:n_win + 1]
    ow = _softmax_av(sw, sw_new, win[:, HEAD_DIM:2 * HEAD_DIM].astype(BF16), wnew[:, HEAD_DIM:2 * HEAD_DIM])
    g = 1.0 / (1.0 + jnp.exp(-gl_ref[0]))
    o_ref[0] = g[:, 0:1] * oc + g[:, 1:2] * osel + g[:, 2:3] * ow


def _decode_call(body, page_table, cache, width, ins, in_specs, out_shape, out_spec, extra_scratch=()):
    b, n_pages = page_table.shape
    page = cache.shape[2]
    grid_spec = pltpu.PrefetchScalarGridSpec(
        num_scalar_prefetch=1, grid=(b,),
        in_specs=list(in_specs) + [pl.BlockSpec(memory_space=pl.ANY)],
        out_specs=out_spec,
        scratch_shapes=[pltpu.VMEM((n_pages * page, width), F32), pltpu.SemaphoreType.DMA((n_pages,))]
        + list(extra_scratch))
    return pl.pallas_call(body, grid_spec=grid_spec, out_shape=out_shape, compiler_params=_cparams(1))(
        page_table.reshape(-1), *ins, cache)


def _row_bias(tab_rows, pos, n_keys, kpos0=0, window=None):
    kpos = jnp.concatenate([kpos0 + jnp.arange(n_keys), jnp.full((1,), pos)])
    rel = pos - kpos
    bias = _bias_lookup(tab_rows.T, rel).T
    if window is not None:
        bias = jnp.where((rel < window)[None, :], bias, NEG)
    bias = jnp.pad(bias, ((0, ROWS - bias.shape[0]), (0, LANE - 1)))
    return bias.astype(F32)


def _pad_rows(a):
    return jnp.pad(a, ((0, 0), (0, ROWS - a.shape[1]), (0, 0)))


def _layer_sample(x, pos, caches, win_state, page_table, layer, lp, prep, rel_bias, lam_init):
    b = x.shape[0]
    n_pages = page_table.shape[1]
    page = caches[0].shape[2]
    tk = n_pages * page
    win_buf = win_state.shape[2]
    assert pos == tk and tk % MOBA_BLOCK == 0 and tk // MOBA_BLOCK <= LANE and tk // NSA_SLC_BLOCK < LANE
    assert tk // 4 >= DSA_TOPK and win_buf == NSA_WINDOW and tk // NSA_CMP_STRIDE == LANE
    w_main, g64, g32, w_gate, u_bf, vt_bf = prep
    (mq, nq, dq, sq, sqi, misc, moba_new, nsa_new, win_new, diff_new, dsa_new) = _in_proj(
        x.reshape(b, D_MODEL), lp['norm_mix'], w_main, g64, g32, b)[:11]
    depth, n_pool = caches[0].shape[0], caches[0].shape[1]
    c_moba = caches[0].reshape(depth, n_pool, page, 2 * MIX_W)
    c_nsa = caches[1].reshape(depth, n_pool, page, 4 * HEAD_DIM)
    c_diff = caches[2].reshape(depth, n_pool, page, 2 * MIX_W)
    c_dsa = caches[3]
    per_seq = lambda *shape: pl.BlockSpec((1,) + shape, lambda i, pt: (i,) + (0,) * len(shape))
    full = lambda a: pl.BlockSpec(a.shape, lambda i, pt: (0,) * a.ndim)
    tabs = [rel_bias[:, 4 * i:4 * i + 4].T for i in range(N_MIXERS)]
    kpos = jnp.arange(tk)
    kw = dict(layer=layer, n_pages=n_pages, page=page)

    bias = _row_bias(tabs[0], pos, tk)
    ex = (kpos[None, :] // MOBA_BLOCK == jnp.arange(LANE)[:, None]).astype(BF16)
    o_a = _decode_call(functools.partial(_moba_dec_kernel, **kw), page_table, c_moba, 2 * MIX_W,
                       (mq.reshape(b, 1, MIX_W), moba_new.reshape(b, 1, 2 * MIX_W), bias, ex),
                       [per_seq(1, MIX_W), per_seq(1, 2 * MIX_W), full(bias), full(ex)],
                       jax.ShapeDtypeStruct((b, 1, MIX_W), F32), per_seq(1, MIX_W))
    n_cmp = (tk + 1 - NSA_CMP_LEN) // NSA_CMP_STRIDE + 1
    cidx = jnp.arange(LANE)
    crel = pos - (cidx * NSA_CMP_STRIDE + NSA_CMP_LEN - 1)
    cbias = jnp.where(((crel >= 0) & (cidx < n_cmp))[None, :], _bias_lookup(tabs[1].T, crel).T, NEG)
    cbias = jnp.pad(cbias, ((0, ROWS - N_HEADS), (0, 0))).astype(F32)
    cstart = cidx * NSA_CMP_STRIDE
    sstart = jnp.arange(LANE) * NSA_SLC_BLOCK
    overlap = ((cstart[:, None] < sstart[None, :] + NSA_SLC_BLOCK) & (cstart[:, None] + NSA_CMP_LEN > sstart[None, :])
               & (cidx < n_cmp)[:, None] & (sstart[None, :] <= tk)).astype(BF16)
    ex_s = (kpos[None, :] // NSA_SLC_BLOCK == jnp.arange(LANE)[:, None]).astype(BF16)
    bias_s = _row_bias(tabs[1], pos, tk)
    wbias = _row_bias(tabs[1], pos, win_buf, kpos0=pos - win_buf, window=NSA_WINDOW)
    glog = _pad_rows(jnp.pad(misc[:, MISC_G:MISC_G + 3 * N_HEADS].reshape(b, N_HEADS, 3), ((0, 0), (0, 0), (0, LANE - 3))))
    pe2 = lp['nsa_pe'].reshape(2, 1, NSA_CMP_LEN * HEAD_DIM)
    w1b, w2b = lp['nsa_w1'].astype(BF16), lp['nsa_w2'].astype(BF16)
    kn = lp['qk_gain'][3].reshape(1, HEAD_DIM)
    win4 = win_state.reshape(win_state.shape[0], b, win_buf, 2 * HEAD_DIM)
    q4 = lambda a: _pad_rows(a.reshape(b, N_HEADS, HEAD_DIM))
    o_b = _decode_call(functools.partial(_nsa_dec_kernel, **kw), page_table, c_nsa, 2 * HEAD_DIM,
                       (q4(nq), glog, nsa_new.reshape(b, 1, 4 * HEAD_DIM), win4, win_new.reshape(b, 1, 2 * HEAD_DIM),
                        pe2, w1b, w2b, kn, cbias, overlap, ex_s, bias_s, wbias),
                       [per_seq(ROWS, HEAD_DIM), per_seq(ROWS, LANE), per_seq(1, 4 * HEAD_DIM),
                        pl.BlockSpec((1, 1, win_buf, 2 * HEAD_DIM), lambda i, pt: (layer, i, 0, 0)),
                        per_seq(1, 2 * HEAD_DIM), full(pe2), full(w1b), full(w2b), full(kn), full(cbias),
                        full(overlap), full(ex_s), full(bias_s), full(wbias)],
                       jax.ShapeDtypeStruct((b, ROWS, HEAD_DIM), F32), per_seq(ROWS, HEAD_DIM),
                       extra_scratch=[pltpu.VMEM((tk, 2 * HEAD_DIM), F32), pltpu.SemaphoreType.DMA((n_pages,)),
                                      pltpu.VMEM((2, tk // NSA_CMP_STRIDE, NSA_CMP_STRIDE * HEAD_DIM), F32)])
    bias_d = _row_bias(jnp.repeat(tabs[2], 2, axis=0), pos, tk)
    dl = lp['diff_lambda'].astype(F32)
    gain = lp['diff_out_gain'].reshape(1, HEAD_DIM)
    o_c = _decode_call(functools.partial(_diff_dec_kernel, lam_init=lam_init, **kw), page_table, c_diff, 2 * MIX_W,
                       (dq.reshape(b, 1, MIX_W), diff_new.reshape(b, 1, 2 * MIX_W), bias_d, dl, gain),
                       [per_seq(1, MIX_W), per_seq(1, 2 * MIX_W), full(bias_d), full(dl), full(gain)],
                       jax.ShapeDtypeStruct((b, 1, MIX_W), F32), per_seq(1, MIX_W))
    bias_i = _row_bias(tabs[3], pos, tk)
    i = np.arange(LANE)
    tri = jnp.asarray(i[:, None] <= i[None, :], BF16)
    qi8 = _pad_rows(sqi.reshape(b, DSA_IDX_HEADS, DSA_IDX_DIM))
    w8 = _pad_rows(misc[:, MISC_W:MISC_W + DSA_IDX_HEADS].reshape(b, DSA_IDX_HEADS, 1))
    o_d = _decode_call(functools.partial(_dsa_dec_kernel, **kw), page_table, c_dsa, DSA_ROW,
                       (q4(sq), qi8, w8, dsa_new.reshape(b, 1, DSA_ROW), bias_i, tri),
                       [per_seq(ROWS, HEAD_DIM), per_seq(ROWS, DSA_IDX_DIM), per_seq(ROWS, 1), per_seq(1, DSA_ROW),
                        full(bias_i), full(tri)],
                       jax.ShapeDtypeStruct((b, ROWS, HEAD_DIM), F32), per_seq(ROWS, HEAD_DIM))
    heads = lambda o: o[:, :N_HEADS].reshape(b, MIX_W)
    x1, hf_bf, qh = _merge(x.reshape(b, D_MODEL), lp['norm_mix'], w_gate, o_a.reshape(b, MIX_W), heads(o_b),
                           o_c.reshape(b, MIX_W), heads(o_d), lp['w_branch'], lp['w_out'], lp['norm_ffn'],
                           lp['peer_wq'], b)
    x2 = _peer(hf_bf, qh, x1, lp['peer_subkeys'], u_bf, vt_bf, b)
    wrows = jnp.concatenate([win_state[layer][:, 1:], win_new.reshape(b, 1, 2, HEAD_DIM)], axis=1)
    return (x2.reshape(b, 1, D_MODEL), moba_new.reshape(b, 1, 2, N_HEADS, HEAD_DIM), nsa_new.reshape(b, 1, 4, HEAD_DIM),
            diff_new.reshape(b, 1, 2, N_HEADS, HEAD_DIM), dsa_new.reshape(b, 1, DSA_ROW), wrows)


def kernel(x_prompt, x_sample, cache_moba, cache_nsa, cache_diff, cache_dsa, state_nsa_win, page_table,
           rel_bias, norm_mix, w_in, qk_gain, nsa_pe, nsa_w1, nsa_w2, diff_qk_gain, diff_lambda,
           diff_out_gain, w_branch, w_out, norm_ffn, peer_wq, peer_subkeys, peer_u, peer_v):
    depth = w_in.shape[0]
    past_len = page_table.shape[1] * cache_moba.shape[2]
    win_buf = state_nsa_win.shape[2]
    assert x_sample.shape[1] == 1 and x_prompt.shape[1] >= win_buf and past_len >= win_buf
    caches = (cache_moba, cache_nsa, cache_diff, cache_dsa)
    yp, ys = x_prompt, x_sample
    rows_p, rows_s = [], []
    for l in range(depth):
        lp = {'norm_mix': norm_mix[l], 'qk_gain': qk_gain[l], 'nsa_pe': nsa_pe[l],
              'nsa_w1': nsa_w1[l], 'nsa_w2': nsa_w2[l], 'diff_lambda': diff_lambda[l],
              'diff_out_gain': diff_out_gain[l], 'w_branch': w_branch[l],
              'w_out': w_out[l], 'norm_ffn': norm_ffn[l], 'peer_wq': peer_wq[l], 'peer_subkeys': peer_subkeys[l]}
        prep = _prep_w_in(w_in[l], qk_gain[l], diff_qk_gain[l]) + (peer_u[l].astype(BF16), _transpose_bf16(peer_v, l, 512))
        lam_init = 0.8 - 0.6 * math.exp(-0.3 * l)
        out_p = _layer_prompt(yp, lp, prep, rel_bias, lam_init)
        out_s = _layer_sample(ys, past_len, caches, state_nsa_win, page_table, l, lp, prep, rel_bias, lam_init)
        yp, ys = out_p[0], out_s[0]
        rows_p.append(out_p[1:5] + (out_p[5][:, -win_buf:],))
        rows_s.append(out_s[1:])

    def stk(rows, i):
        return jnp.stack([r[i] for r in rows], axis=0)

    return (yp, ys, stk(rows_p, 0), stk(rows_s, 0), stk(rows_p, 1), stk(rows_s, 1),
            stk(rows_p, 2), stk(rows_s, 2), stk(rows_p, 3), stk(rows_s, 3), stk(rows_p, 4), stk(rows_s, 4))
```

```python
import functools
import math

import jax
import jax.numpy as jnp
import numpy as np
from jax import lax
from jax.experimental import pallas as pl
from jax.experimental.pallas import tpu as pltpu

D_MODEL = 1024
N_MIXERS = 4
N_HEADS = 4
HEAD_DIM = 64
MIX_W = N_HEADS * HEAD_DIM
REL_BUCKETS = 32
REL_MAX_DIST = 128
MOBA_BLOCK = 256
MOBA_TOPK = 3
MOBA_QB = 32
NSA_CMP_LEN = 32
NSA_CMP_STRIDE = 16
NSA_CMP_HID = 128
NSA_SLC_BLOCK = 64
NSA_N_SEL = 16
NSA_WINDOW = 512
DIFF_DH = 32
DSA_IDX_HEADS = 4
DSA_IDX_DIM = 32
DSA_TOPK = 256
DSA_ROW = 2 * HEAD_DIM + DSA_IDX_DIM
PEER_HEADS = 8
PEER_KEYS = 128
PEER_TOPK = 16
PEER_QDIM = 256
PEER_CHUNK = 128
Q_BLOCK = 128
EPS = 1e-6
NEG = -1e30
BIG = 1e9

IN_SPLITS = (
    ('moba_q', MIX_W), ('moba_k', MIX_W), ('moba_v', MIX_W),
    ('nsa_q', MIX_W), ('nsa_kc', HEAD_DIM), ('nsa_vc', HEAD_DIM), ('nsa_ks', HEAD_DIM),
    ('nsa_vs', HEAD_DIM), ('nsa_kw', HEAD_DIM), ('nsa_vw', HEAD_DIM), ('nsa_g', 3 * N_HEADS),
    ('diff_q', N_HEADS * 2 * DIFF_DH), ('diff_k', N_HEADS * 2 * DIFF_DH), ('diff_v', N_HEADS * 2 * DIFF_DH),
    ('dsa_q', MIX_W), ('dsa_k', HEAD_DIM), ('dsa_v', HEAD_DIM),
    ('dsa_qi', DSA_IDX_HEADS * DSA_IDX_DIM), ('dsa_ki', DSA_IDX_DIM), ('dsa_w', DSA_IDX_HEADS),
    ('gates', N_MIXERS * D_MODEL),
)
IN_COLS = sum(s for _, s in IN_SPLITS)
IN_OFF = {}
_o = 0
for _n, _s in IN_SPLITS:
    IN_OFF[_n] = (_o, _s)
    _o += _s

LANE = 128
TILE = 256
VMEM_LIMIT = 56 * 1024 * 1024
F32 = jnp.float32
BF16 = jnp.bfloat16
INT_MIN = -2 ** 31

N64_W = 1280
N32_W = 512
RAW_W = 1152
MAIN_W = N64_W + N32_W + RAW_W
MISC_G = 0
MISC_W = 12


def _dot(a, b):
    return jnp.dot(a, b, preferred_element_type=F32)


def _dot_nt(a, b):
    return lax.dot_general(a, b, (((1,), (1,)), ((), ())), preferred_element_type=F32)


def _cparams(n_grid):
    return pltpu.CompilerParams(dimension_semantics=("arbitrary",) * n_grid, vmem_limit_bytes=VMEM_LIMIT)


def _t5_bucket(rel):
    n = jnp.maximum(rel, 0)
    exact = REL_BUCKETS // 2
    nf = jnp.maximum(n, 1).astype(jnp.float32)
    large = exact + (jnp.log(nf / exact) / math.log(REL_MAX_DIST / exact) * (REL_BUCKETS - exact)).astype(jnp.int32)
    return jnp.where(n < exact, n, jnp.minimum(large, REL_BUCKETS - 1))


def _bias_lookup(tab, rel):
    bucket = _t5_bucket(rel)[..., None]
    out = jnp.broadcast_to(tab[REL_BUCKETS - 1], bucket.shape[:-1] + (tab.shape[1],))
    for k in range(REL_BUCKETS - 1):
        out = jnp.where(bucket == k, tab[k], out)
    return out


def _bias_tiles(tab, tq, tk, n_d, window=None):
    r = jnp.arange(tq)[:, None]
    c = jnp.arange(tk)[None, :]
    rel = jnp.arange(n_d)[:, None, None] * tq + r - c
    ok = rel >= 0
    if window is not None:
        ok = ok & (rel < window)
    b = jnp.where(ok[..., None], _bias_lookup(tab, rel), NEG)
    return jnp.transpose(b, (3, 0, 1, 2)).astype(F32)


def _prep_w_in(w_in, qk_gain, diff_qk_gain):
    def col(name):
        o, s = IN_OFF[name]
        return w_in[:, o:o + s]

    z = lambda n: jnp.zeros((D_MODEL, n), w_in.dtype)
    w_main = jnp.concatenate([
        col('moba_q'), col('moba_k'), col('nsa_q'), col('dsa_q'), col('nsa_ks'), col('nsa_kw'), col('dsa_k'), z(64),
        col('diff_q'), col('diff_k'),
        col('moba_v'), col('diff_v'), col('nsa_kc'), col('nsa_vc'), col('nsa_vs'), col('nsa_vw'), col('dsa_v'),
        col('dsa_ki'), z(32), col('dsa_qi'), col('nsa_g'), col('dsa_w'), z(LANE - 16)], axis=1).astype(BF16)
    t4 = lambda g: jnp.tile(g, N_HEADS)
    g64 = jnp.concatenate([t4(qk_gain[0]), t4(qk_gain[1]), t4(qk_gain[2]), t4(qk_gain[6]),
                           qk_gain[4], qk_gain[5], qk_gain[7], jnp.zeros((64,), F32)]).reshape(1, N64_W)
    g32 = jnp.concatenate([jnp.tile(diff_qk_gain[0], 8), jnp.tile(diff_qk_gain[1], 8)]).reshape(1, N32_W)
    w_gate = col('gates').astype(BF16)
    return w_main, g64, g32, w_gate


def _block_diag_ones(seg):
    i = np.arange(LANE)
    return jnp.asarray((i[:, None] // seg) == (i[None, :] // seg), BF16)


def _inproj_kernel(x_ref, g_ref, w_ref, g64_ref, g32_ref, bd64_ref, bd32_ref,
                   mq_ref, nq_ref, dq_ref, sq_ref, sqi_ref, misc_ref,
                   moba_ref, nsa_ref, win_ref, diff_ref, dsa_ref, kmean_ref, vtm_ref, vtd_ref, vts_ref):
    x = x_ref[...]
    hn = (x * lax.rsqrt(jnp.mean(x * x, axis=-1, keepdims=True) + EPS) * g_ref[...]).astype(BF16)

    def segnorm(p, bd, seg, gain):
        x2 = p * p
        hi = x2.astype(BF16)
        lo = (x2 - hi.astype(F32)).astype(BF16)
        ss = _dot(hi, bd) + _dot(lo, bd)
        return p * lax.rsqrt(ss * (1.0 / seg) + EPS) * gain

    p1 = _dot(hn, w_ref[:, 0:N64_W])
    bd64 = bd64_ref[...]
    c = [segnorm(p1[:, i * LANE:(i + 1) * LANE], bd64, 64, g64_ref[:, i * LANE:(i + 1) * LANE])
         for i in range(N64_W // LANE)]
    mq_ref[:, 0:128] = c[0]
    mq_ref[:, 128:256] = c[1]
    moba_ref[:, 0:128] = c[2]
    moba_ref[:, 128:256] = c[3]
    kmean_ref[0] = jnp.concatenate([jnp.mean(c[2], axis=0, keepdims=True),
                                    jnp.mean(c[3], axis=0, keepdims=True)], axis=1)
    nq_ref[:, 0:128] = c[4]
    nq_ref[:, 128:256] = c[5]
    sq_ref[:, 0:128] = c[6]
    sq_ref[:, 128:256] = c[7]
    nsa_ref[:, 128:192] = c[8][:, 0:64]
    win_ref[:, 0:64] = c[8][:, 64:128]
    dsa_ref[:, 0:64] = c[9][:, 0:64]

    p2 = _dot(hn, w_ref[:, N64_W:N64_W + N32_W])
    bd32 = bd32_ref[...]
    d = [segnorm(p2[:, i * LANE:(i + 1) * LANE], bd32, 32, g32_ref[:, i * LANE:(i + 1) * LANE])
         for i in range(N32_W // LANE)]
    dq_ref[:, 0:128] = d[0]
    dq_ref[:, 128:256] = d[1]
    diff_ref[:, 0:128] = d[2]
    diff_ref[:, 128:256] = d[3]

    p3 = _dot(hn, w_ref[:, N64_W + N32_W:MAIN_W])
    moba_ref[:, 256:512] = p3[:, 0:256]
    diff_ref[:, 256:512] = p3[:, 256:512]
    nsa_ref[:, 0:128] = p3[:, 512:640]
    nsa_ref[:, 192:256] = p3[:, 640:704]
    win_ref[:, 64:128] = p3[:, 704:768]
    dsa_ref[:, 64:128] = p3[:, 768:832]
    dsa_ref[:, 128:160] = p3[:, 832:864]
    sqi_ref[...] = p3[:, 896:1024]
    misc_ref[...] = p3[:, 1024:1152]
    vtm_ref[0] = p3[:, 0:256].T.astype(BF16)
    vtd_ref[0] = p3[:, 256:512].T.astype(BF16)
    vts_ref[0] = p3[:, 640:896].T.astype(BF16)


def _in_proj(x2d, norm_g, w_main, g64, g32, tm):
    n = x2d.shape[0]
    row = lambda w: pl.BlockSpec((tm, w), lambda i: (i, 0))
    full = lambda a: pl.BlockSpec(a.shape, lambda i: (0,) * a.ndim)
    bd64, bd32 = _block_diag_ones(64), _block_diag_ones(32)
    g = norm_g.reshape(1, D_MODEL)
    widths = (MIX_W, MIX_W, MIX_W, MIX_W, 128, 128, 512, 256, 128, 512, DSA_ROW)
    out_shape = [jax.ShapeDtypeStruct((n, w), F32) for w in widths]
    out_shape.append(jax.ShapeDtypeStruct((n // tm, 1, MIX_W), F32))
    out_shape += [jax.ShapeDtypeStruct((n // tm, MIX_W, tm), BF16)] * 3
    out_specs = ([row(w) for w in widths] + [pl.BlockSpec((1, 1, MIX_W), lambda i: (i, 0, 0))]
                 + [pl.BlockSpec((1, MIX_W, tm), lambda i: (i, 0, 0))] * 3)
    return pl.pallas_call(
        _inproj_kernel,
        grid=(n // tm,),
        in_specs=[row(D_MODEL), full(g), full(w_main), full(g64), full(g32), full(bd64), full(bd32)],
        out_specs=out_specs,
        out_shape=out_shape,
        compiler_params=_cparams(1),
    )(x2d, g, w_main, g64, g32, bd64, bd32)


def _flash_init(m_sc, l_sc, acc_sc):
    m_sc[...] = jnp.full(m_sc.shape, NEG, F32)
    l_sc[...] = jnp.zeros(l_sc.shape, F32)
    acc_sc[...] = jnp.zeros(acc_sc.shape, F32)


def _flash_update_t(st, i, vt_bf, m_sc, l_sc, acc_sc):
    m_old = m_sc[i]
    m_new = jnp.maximum(m_old, jnp.max(st, axis=0, keepdims=True))
    alpha = jnp.exp(m_old - m_new)
    p = jnp.exp(st - m_new)
    l_sc[i] = alpha * l_sc[i] + jnp.sum(p, axis=0, keepdims=True)
    acc_sc[i] = alpha * acc_sc[i] + _dot(vt_bf, p.astype(BF16))
    m_sc[i] = m_new


def _flash_scratch_t(n_streams, tq, dv):
    return [pltpu.VMEM((n_streams, 1, tq), F32), pltpu.VMEM((n_streams, 1, tq), F32),
            pltpu.VMEM((n_streams, dv, tq), F32)]


def _bias_tiles_t(tab, tq, n_d, window=None):
    return jnp.swapaxes(_bias_tiles(tab, tq, tq, n_d, window), 2, 3)


def _transpose_kernel(x_ref, o_ref):
    o_ref[...] = x_ref[0].T.astype(BF16)


def _transpose_bf16(x, layer, te):
    _, e, d = x.shape
    return pl.pallas_call(
        _transpose_kernel,
        grid=(e // te,),
        in_specs=[pl.BlockSpec((1, te, d), lambda i: (layer, i, 0))],
        out_specs=pl.BlockSpec((d, te), lambda i: (0, i)),
        out_shape=jax.ShapeDtypeStruct((d, e), BF16),
        compiler_params=_cparams(1),
    )(x)


def _diff_kernel(q_ref, k_ref, v_ref, bt_ref, dl_ref, gain_ref, o_ref, m_sc, l_sc, acc_sc, *, lam_init):
    qi = pl.program_id(1)
    tq = q_ref.shape[1]
    scale = DIFF_DH ** -0.5
    q = q_ref[0].astype(BF16)
    _flash_init(m_sc, l_sc, acc_sc)

    def body(kj, carry):
        d = jnp.minimum(qi - kj, 2)
        off = pl.multiple_of(kj * tq, tq)
        k = k_ref[0, pl.ds(off, tq), :]
        vt = v_ref[0, kj]
        for h in range(N_HEADS):
            bias = bt_ref[h, d]
            vh = vt[h * HEAD_DIM:(h + 1) * HEAD_DIM, :]
            for m in range(2):
                lo = h * HEAD_DIM + m * DIFF_DH
                st = _dot_nt(k[:, lo:lo + DIFF_DH], q[:, lo:lo + DIFF_DH]) * scale + bias
                _flash_update_t(st, 2 * h + m, vh, m_sc, l_sc, acc_sc)
        return carry

    lax.fori_loop(0, qi + 1, body, 0)
    dl = dl_ref[...]
    lam = (jnp.exp(jnp.sum(dl[0:1] * dl[1:2], axis=-1, keepdims=True))
           - jnp.exp(jnp.sum(dl[2:3] * dl[3:4], axis=-1, keepdims=True)) + lam_init)
    outs = []
    for h in range(N_HEADS):
        o = acc_sc[2 * h] / l_sc[2 * h] - lam * (acc_sc[2 * h + 1] / l_sc[2 * h + 1])
        outs.append(o * lax.rsqrt(jnp.mean(o * o, axis=0, keepdims=True) + EPS) * gain_ref[...] * (1.0 - lam_init))
    o_ref[0] = jnp.concatenate(outs, axis=0).T


def _diff_prompt(dq, dk_bf, dv_t, tab, dl, out_gain, lam_init):
    b, t, _ = dq.shape
    tq = TILE
    bt = _bias_tiles_t(tab, tq, 3)
    gain = out_gain.reshape(HEAD_DIM, 1)
    full = lambda a: pl.BlockSpec(a.shape, lambda i, j: (0,) * a.ndim)
    return pl.pallas_call(
        functools.partial(_diff_kernel, lam_init=lam_init),
        grid=(b, t // tq),
        in_specs=[pl.BlockSpec((1, tq, MIX_W), lambda i, j: (i, j, 0)),
                  pl.BlockSpec((1, t, MIX_W), lambda i, j: (i, 0, 0)),
                  pl.BlockSpec((1, t // tq, MIX_W, tq), lambda i, j: (i, 0, 0, 0)),
                  full(bt), full(dl), full(gain)],
        out_specs=pl.BlockSpec((1, tq, MIX_W), lambda i, j: (i, j, 0)),
        out_shape=jax.ShapeDtypeStruct((b, t, MIX_W), F32),
        scratch_shapes=_flash_scratch_t(2 * N_HEADS, tq, HEAD_DIM),
        compiler_params=_cparams(2),
    )(dq, dk_bf, dv_t, bt, dl, gain)


def _moba_kernel(q_ref, k_ref, v_ref, km_ref, bt_ref, o_ref, sel_sc, m_sc, l_sc, acc_sc):
    qi = pl.program_id(1)
    tq = q_ref.shape[1]
    scale = HEAD_DIM ** -0.5
    q = q_ref[0].astype(BF16)
    km = km_ref[0].astype(BF16)
    jidx = lax.broadcasted_iota(jnp.int32, (LANE, tq), 0)
    _flash_init(m_sc, l_sc, acc_sc)
    for h in range(N_HEADS):
        hs = slice(h * HEAD_DIM, (h + 1) * HEAD_DIM)
        g = _dot_nt(km[:, hs], q[:, hs])
        g = jnp.where(jidx < qi, g, -BIG)
        sel = jidx == qi
        for _ in range(MOBA_TOPK):
            mx = jnp.max(g, axis=0, keepdims=True)
            first = jnp.min(jnp.where(g == mx, jidx, LANE), axis=0, keepdims=True)
            hit = jidx == first
            sel = sel | (hit & (mx > -0.5 * BIG))
            g = jnp.where(hit, -3.0 * BIG, g)
        sel_sc[h] = sel.astype(F32)

    def body(kj, carry):
        d = jnp.minimum(qi - kj, 2)
        off = pl.multiple_of(kj * tq, tq)
        k = k_ref[0, pl.ds(off, tq), :]
        vt = v_ref[0, kj]
        for h in range(N_HEADS):
            hs = slice(h * HEAD_DIM, (h + 1) * HEAD_DIM)
            colsel = sel_sc[h, pl.ds(kj, 1), :] > 0.5
            st = _dot_nt(k[:, hs], q[:, hs]) * scale + bt_ref[h, d]
            st = jnp.where(colsel, st, NEG)
            _flash_update_t(st, h, vt[hs, :], m_sc, l_sc, acc_sc)
        return carry

    lax.fori_loop(0, qi + 1, body, 0)
    o_ref[0] = jnp.concatenate([acc_sc[h] / l_sc[h] for h in range(N_HEADS)], axis=0).T


def _moba_prompt(mq, k_bf, v_t, kmean, tab):
    b, t, _ = mq.shape
    tq = TILE
    assert tq == MOBA_BLOCK and t % tq == 0 and t // tq <= LANE
    bt = _bias_tiles_t(tab, tq, 3)
    km = jnp.pad(kmean, ((0, 0), (0, LANE - kmean.shape[1]), (0, 0)))
    full = lambda a: pl.BlockSpec(a.shape, lambda i, j: (0,) * a.ndim)
    return pl.pallas_call(
        _moba_kernel,
        grid=(b, t // tq),
        in_specs=[pl.BlockSpec((1, tq, MIX_W), lambda i, j: (i, j, 0)),
                  pl.BlockSpec((1, t, MIX_W), lambda i, j: (i, 0, 0)),
                  pl.BlockSpec((1, t // tq, MIX_W, tq), lambda i, j: (i, 0, 0, 0)),
                  pl.BlockSpec((1, LANE, MIX_W), lambda i, j: (i, 0, 0)),
                  full(bt)],
        out_specs=pl.BlockSpec((1, tq, MIX_W), lambda i, j: (i, j, 0)),
        out_shape=jax.ShapeDtypeStruct((b, t, MIX_W), F32),
        scratch_shapes=[pltpu.VMEM((N_HEADS, LANE, tq), F32)] + _flash_scratch_t(N_HEADS, tq, HEAD_DIM),
        compiler_params=_cparams(2),
    )(mq, k_bf, v_t, km, bt)


def _sortable(x):
    x = jnp.where(x == 0.0, 0.0, x)
    bits = pltpu.bitcast(x, jnp.int32)
    return bits ^ ((bits >> 31) & 0x7FFFFFFF)


def _dsa_kernel(q_ref, qi_ref, misc_ref, k_ref, v_ref, ki_ref, bt_ref, tri_ref, o_ref,
                key_sc, m_sc, l_sc, acc_sc):
    qi = pl.program_id(1)
    tq = q_ref.shape[1]
    scale = HEAD_DIM ** -0.5
    q = q_ref[0].astype(BF16)
    qidx = qi_ref[0].astype(BF16)
    wt = misc_ref[0].T[MISC_W:MISC_W + DSA_IDX_HEADS, :]
    k_io = lax.broadcasted_iota(jnp.int32, (tq, tq), 0)
    q_io = lax.broadcasted_iota(jnp.int32, (tq, tq), 1)
    nk = qi + 1

    def score_body(kj, carry):
        off = pl.multiple_of(kj * tq, tq)
        ki = ki_ref[0, pl.ds(off, tq), :]
        sc = jnp.zeros((tq, tq), F32)
        for h in range(DSA_IDX_HEADS):
            idx = _dot_nt(ki, qidx[:, h * DSA_IDX_DIM:(h + 1) * DSA_IDX_DIM])
            sc = sc + jnp.maximum(idx, 0.0) * wt[h:h + 1, :]
        sc = jnp.where((kj < qi) | (k_io <= q_io), sc, -BIG)
        key_sc[kj] = _sortable(sc)
        return carry

    lax.fori_loop(0, nk, score_body, 0)

    def count_ge(tau):
        def cb(kj, cnt):
            hit = jnp.where(key_sc[kj] >= tau, 1.0, 0.0)
            return cnt + jnp.sum(hit.reshape(tq // ROWS, ROWS, tq), axis=0)
        return jnp.sum(lax.fori_loop(0, nk, cb, jnp.zeros((ROWS, tq), F32)), axis=0, keepdims=True)

    def bis_body(it, tau):
        cand = tau + lax.shift_left(jnp.int32(1), 31 - it)
        return jnp.where(count_ge(cand) >= float(DSA_TOPK), cand, tau)

    tau = lax.fori_loop(0, 32, bis_body, jnp.full((1, tq), INT_MIN, jnp.int32))
    n_gt = count_ge(tau + 1)
    need = float(DSA_TOPK) - n_gt
    tri = tri_ref[...]
    _flash_init(m_sc, l_sc, acc_sc)

    def att_body(kj, run):
        d = jnp.minimum(qi - kj, 2)
        off = pl.multiple_of(kj * tq, tq)
        key = key_sc[kj]
        eq = key == tau
        pref = run + _dot(tri, jnp.where(eq, 1.0, 0.0).astype(BF16))
        mask = (key > tau) | (eq & (pref <= need))
        k = k_ref[0, pl.ds(off, tq), :]
        vt = v_ref[0, kj]
        for h in range(N_HEADS):
            st = _dot_nt(k, q[:, h * HEAD_DIM:(h + 1) * HEAD_DIM]) * scale + bt_ref[h, d]
            st = jnp.where(mask, st, NEG)
            _flash_update_t(st, h, vt, m_sc, l_sc, acc_sc)
        return run + jnp.sum(jnp.where(eq, 1.0, 0.0), axis=0, keepdims=True)

    lax.fori_loop(0, nk, att_body, jnp.zeros((1, tq), F32))
    o_ref[0] = jnp.concatenate([acc_sc[h] / l_sc[h] for h in range(N_HEADS)], axis=0).T


def _dsa_prompt(sq, sqi, misc, k_bf, v_t, v_row, ki_bf, tab):
    b, t, _ = sq.shape
    tq = TILE
    assert tq >= DSA_TOPK and t // 4 >= DSA_TOPK
    bt = _bias_tiles_t(tab, tq, 3)
    i = np.arange(tq)
    tri = jnp.asarray(i[:, None] >= i[None, :], BF16)
    full = lambda a: pl.BlockSpec(a.shape, lambda i, j: (0,) * a.ndim)
    seq = lambda w: pl.BlockSpec((1, t, w), lambda i, j: (i, 0, 0))
    til = lambda w: pl.BlockSpec((1, tq, w), lambda i, j: (i, j, 0))
    return pl.pallas_call(
        _dsa_kernel,
        grid=(b, t // tq),
        in_specs=[til(MIX_W), til(128), til(128), seq(HEAD_DIM),
                  pl.BlockSpec((1, t // tq, HEAD_DIM, tq), lambda i, j: (i, 0, v_row, 0)),
                  seq(DSA_IDX_DIM), full(bt), full(tri)],
        out_specs=til(MIX_W),
        out_shape=jax.ShapeDtypeStruct((b, t, MIX_W), F32),
        scratch_shapes=[pltpu.VMEM((t // tq, tq, tq), jnp.int32)] + _flash_scratch_t(N_HEADS, tq, HEAD_DIM),
        compiler_params=_cparams(2),
    )(sq, sqi, misc, k_bf, v_t, ki_bf, bt, tri)


def _compress_kernel(c_ref, pe_ref, w1_ref, w2_ref, g_ref, o_ref):
    for i in range(2):
        o_ref[0, i] = _compress_math(c_ref[0, i], i, pe_ref, w1_ref, w2_ref, g_ref)


def _nsa_compress(kc, vc, pe, w1, w2, kn_gain):
    b, tc, _ = kc.shape
    nch = tc // NSA_CMP_STRIDE
    c = jnp.stack([kc, vc], axis=1).reshape(b, 2, nch, NSA_CMP_STRIDE * HEAD_DIM)
    pe2 = pe.reshape(2, 1, NSA_CMP_LEN * HEAD_DIM)
    full = lambda a: pl.BlockSpec(a.shape, lambda i: (0,) * a.ndim)
    g = kn_gain.reshape(1, HEAD_DIM)
    w1b, w2b = w1.astype(BF16), w2.astype(BF16)
    return pl.pallas_call(
        _compress_kernel,
        grid=(b,),
        in_specs=[pl.BlockSpec((1, 2, nch, NSA_CMP_STRIDE * HEAD_DIM), lambda i: (i, 0, 0, 0)),
                  full(pe2), full(w1b), full(w2b), full(g)],
        out_specs=pl.BlockSpec((1, 2, nch, HEAD_DIM), lambda i: (i, 0, 0, 0)),
        out_shape=jax.ShapeDtypeStruct((b, 2, nch, HEAD_DIM), F32),
        compiler_params=_cparams(1),
    )(c, pe2, w1b, w2b, g)


def _nsa_kernel(q_ref, misc_ref, kcmp_ref, vcmpt_ref, ks_ref, vs_ref, kw_ref, vw_ref, cb_ref, ov_ref, ex_ref,
                bt_ref, wbt_ref, o_ref, m_sc, l_sc, acc_sc, oc_sc):
    qi = pl.program_id(1)
    tq = q_ref.shape[1]
    n_slc = ov_ref.shape[0]
    scale = HEAD_DIM ** -0.5
    q = q_ref[0].astype(BF16)
    kcmp = kcmp_ref[0].astype(BF16)
    vcmpt = vcmpt_ref[0].astype(BF16)
    ovt = ov_ref[...]

    imp = jnp.zeros((n_slc, tq), F32)
    for h in range(N_HEADS):
        cb = cb_ref[h]
        ok = cb > 0.5 * NEG
        z = jnp.where(ok, _dot_nt(kcmp, q[:, h * HEAD_DIM:(h + 1) * HEAD_DIM]) * scale + cb, NEG)
        z = z - jnp.max(z, axis=0, keepdims=True)
        p = jnp.where(ok, jnp.exp(z), 0.0)
        p = p / jnp.maximum(jnp.sum(p, axis=0, keepdims=True), 1e-30)
        pb = p.astype(BF16)
        oc_sc[h] = _dot(vcmpt, pb)
        imp = imp + _dot(ovt, pb)

    jidx = lax.broadcasted_iota(jnp.int32, (n_slc, tq), 0)
    pos = qi * tq + lax.broadcasted_iota(jnp.int32, (n_slc, tq), 1)
    bt = pos // NSA_SLC_BLOCK
    forced = (jidx == 0) | (jidx == bt) | (jidx == bt - 1)
    score = jnp.where(jidx <= bt, jnp.where(forced, BIG, imp), -BIG)
    sel = jnp.zeros((n_slc, tq), F32)
    for _ in range(min(NSA_N_SEL, n_slc)):
        mx = jnp.max(score, axis=0, keepdims=True)
        first = jnp.min(jnp.where(score == mx, jidx, n_slc), axis=0, keepdims=True)
        hit = jidx == first
        sel = jnp.where(hit, 1.0, sel)
        score = jnp.where(hit, -3.0 * BIG, score)
    sel_bf = sel.astype(BF16)

    _flash_init(m_sc, l_sc, acc_sc)

    def slc_body(kj, carry):
        d = jnp.minimum(qi - kj, 2)
        off = pl.multiple_of(kj * tq, tq)
        mask = _dot(ex_ref[kj], sel_bf) > 0.5
        k = ks_ref[0, pl.ds(off, tq), :]
        vt = vs_ref[0, kj]
        for h in range(N_HEADS):
            st = _dot_nt(k, q[:, h * HEAD_DIM:(h + 1) * HEAD_DIM]) * scale + bt_ref[h, d]
            st = jnp.where(mask, st, NEG)
            _flash_update_t(st, h, vt, m_sc, l_sc, acc_sc)
        return carry

    lax.fori_loop(0, qi + 1, slc_body, 0)

    n_w = wbt_ref.shape[1]

    def win_body(kj, carry):
        d = qi - kj
        off = pl.multiple_of(kj * tq, tq)
        k = kw_ref[0, pl.ds(off, tq), :]
        vt = vw_ref[0, kj]
        for h in range(N_HEADS):
            st = _dot_nt(k, q[:, h * HEAD_DIM:(h + 1) * HEAD_DIM]) * scale + wbt_ref[h, d]
            _flash_update_t(st, N_HEADS + h, vt, m_sc, l_sc, acc_sc)
        return carry

    lax.fori_loop(jnp.maximum(qi - (n_w - 1), 0), qi + 1, win_body, 0)

    gl = misc_ref[0].T[MISC_G:MISC_G + 3 * N_HEADS, :]
    g = 1.0 / (1.0 + jnp.exp(-gl))
    outs = []
    for h in range(N_HEADS):
        outs.append(g[3 * h:3 * h + 1] * oc_sc[h]
                    + g[3 * h + 1:3 * h + 2] * (acc_sc[h] / l_sc[h])
                    + g[3 * h + 2:3 * h + 3] * (acc_sc[N_HEADS + h] / l_sc[N_HEADS + h]))
    o_ref[0] = jnp.concatenate(outs, axis=0).T


def _nsa_prompt(nq, misc, cmp_kv, ks_bf, kw_bf, v_t, vs_row, vw_row, tab):
    b, t, _ = nq.shape
    tq = TILE
    n_cmp = (t - NSA_CMP_LEN) // NSA_CMP_STRIDE + 1
    n_cpad = cmp_kv.shape[2]
    n_slc = t // NSA_SLC_BLOCK
    assert t % tq == 0 and tq % NSA_SLC_BLOCK == 0 and n_cpad >= n_cmp
    kcmp = cmp_kv[:, 0]
    vcmpt = jnp.swapaxes(cmp_kv[:, 1], 1, 2)
    cidx = jnp.arange(n_cpad)
    crel = jnp.arange(t)[None, :] - (cidx * NSA_CMP_STRIDE + NSA_CMP_LEN - 1)[:, None]
    cok = (crel >= 0) & (cidx < n_cmp)[:, None]
    cbias = jnp.transpose(jnp.where(cok[..., None], _bias_lookup(tab, crel), NEG), (2, 0, 1)).astype(F32)
    cstart = cidx * NSA_CMP_STRIDE
    sstart = jnp.arange(n_slc) * NSA_SLC_BLOCK
    overlap_t = ((cstart[None, :] < sstart[:, None] + NSA_SLC_BLOCK)
                 & (cstart[None, :] + NSA_CMP_LEN > sstart[:, None]) & (cidx < n_cmp)[None, :]).astype(BF16)
    kpos = jnp.arange(t).reshape(t // tq, tq, 1)
    expand_t = (kpos // NSA_SLC_BLOCK == jnp.arange(n_slc)[None, None, :]).astype(BF16)
    bt = _bias_tiles_t(tab, tq, 3)
    n_w = (NSA_WINDOW - 1 + tq - 1) // tq + 1
    wbt = _bias_tiles_t(tab, tq, n_w, window=NSA_WINDOW)
    full = lambda a: pl.BlockSpec(a.shape, lambda i, j: (0,) * a.ndim)
    seq = lambda w: pl.BlockSpec((1, t, w), lambda i, j: (i, 0, 0))
    seq_t = lambda r: pl.BlockSpec((1, t // tq, HEAD_DIM, tq), lambda i, j: (i, 0, r, 0))
    til = lambda w: pl.BlockSpec((1, tq, w), lambda i, j: (i, j, 0))
    return pl.pallas_call(
        _nsa_kernel,
        grid=(b, t // tq),
        in_specs=[til(MIX_W), til(128),
                  pl.BlockSpec((1, n_cpad, HEAD_DIM), lambda i, j: (i, 0, 0)),
                  pl.BlockSpec((1, HEAD_DIM, n_cpad), lambda i, j: (i, 0, 0)),
                  seq(HEAD_DIM), seq_t(vs_row), seq(HEAD_DIM), seq_t(vw_row),
                  pl.BlockSpec((N_HEADS, n_cpad, tq), lambda i, j: (0, 0, j)),
                  full(overlap_t), full(expand_t), full(bt), full(wbt)],
        out_specs=til(MIX_W),
        out_shape=jax.ShapeDtypeStruct((b, t, MIX_W), F32),
        scratch_shapes=_flash_scratch_t(2 * N_HEADS, tq, HEAD_DIM) + [pltpu.VMEM((N_HEADS, HEAD_DIM, tq), F32)],
        compiler_params=_cparams(2),
    )(nq, misc, kcmp, vcmpt, ks_bf, v_t, kw_bf, v_t, cbias, overlap_t, expand_t, bt, wbt)


def _merge_kernel(x_ref, g_ref, wg_ref, oa_ref, ob_ref, oc_ref, od_ref, wb_ref, wo_ref, gf_ref, wq_ref,
                  x1_ref, hf_ref, qh_ref):
    x = x_ref[...]
    hn = (x * lax.rsqrt(jnp.mean(x * x, axis=-1, keepdims=True) + EPS) * g_ref[...]).astype(BF16)
    merged = jnp.zeros(x.shape, F32)
    for i, o_ref in enumerate((oa_ref, ob_ref, oc_ref, od_ref)):
        gl = _dot(hn, wg_ref[:, i * D_MODEL:(i + 1) * D_MODEL])
        z = _dot(o_ref[...].astype(BF16), wb_ref[i])
        merged = merged + z / (1.0 + jnp.exp(-gl))
    x1 = x + _dot(merged.astype(BF16), wo_ref[...])
    x1_ref[...] = x1
    hf = (x1 * lax.rsqrt(jnp.mean(x1 * x1, axis=-1, keepdims=True) + EPS) * gf_ref[...])
    hfb = hf.astype(BF16)
    hf_ref[...] = hfb
    qh_ref[...] = _dot(hfb, wq_ref[...])


def _merge(x2d, norm_g, w_gate, o_a, o_b, o_c, o_d, w_branch, w_out, norm_ffn, wq, tm):
    n = x2d.shape[0]
    row = lambda w: pl.BlockSpec((tm, w), lambda i: (i, 0))
    full = lambda a: pl.BlockSpec(a.shape, lambda i: (0,) * a.ndim, pipeline_mode=pl.Buffered(1))
    g = norm_g.reshape(1, D_MODEL)
    gf = norm_ffn.reshape(1, D_MODEL)
    wb = w_branch.astype(BF16)
    wo = w_out.astype(BF16)
    wqb = wq.astype(BF16)
    nq = wq.shape[1]
    return pl.pallas_call(
        _merge_kernel,
        grid=(n // tm,),
        in_specs=[row(D_MODEL), full(g), full(w_gate), row(MIX_W), row(MIX_W), row(MIX_W), row(MIX_W),
                  full(wb), full(wo), full(gf), full(wqb)],
        out_specs=[row(D_MODEL), row(D_MODEL), row(nq)],
        out_shape=[jax.ShapeDtypeStruct((n, D_MODEL), F32), jax.ShapeDtypeStruct((n, D_MODEL), BF16),
                   jax.ShapeDtypeStruct((n, nq), F32)],
        compiler_params=_cparams(1),
    )(x2d, g, w_gate, o_a, o_b, o_c, o_d, wb, wo, gf, wqb)


PEER_I1_CHUNK = 8
PEER_ECHUNK = PEER_I1_CHUNK * PEER_KEYS


def _top_desc(x, n, iota0):
    rows = []
    big = x.shape[0]
    for _ in range(n):
        mx = jnp.max(x, axis=0, keepdims=True)
        first = jnp.min(jnp.where(x == mx, iota0, big), axis=0, keepdims=True)
        x = jnp.where(iota0 == first, -jnp.inf, x)
        rows.append(mx)
    return rows


def _peer_kernel(hf_ref, qh_ref, sk_ref, u_ref, vt_ref, x1_ref, o_ref, s1_sc, s2_sc, e1_sc, e2_sc, tau_sc, acc_sc):
    c = pl.program_id(1)
    tm = hf_ref.shape[0]
    half = PEER_QDIM // 2

    @pl.when(c == 0)
    def _():
        acc_sc[...] = jnp.zeros(acc_sc.shape, F32)
        io_k = lax.broadcasted_iota(jnp.int32, (PEER_KEYS, tm), 0)
        n_cand = sum(PEER_TOPK // (a + 1) for a in range(PEER_TOPK))
        n_cand_pad = -n_cand % 8
        io_c = lax.broadcasted_iota(jnp.int32, (n_cand + n_cand_pad, tm), 0)
        for h in range(PEER_HEADS):
            qh = qh_ref[:, h * PEER_QDIM:(h + 1) * PEER_QDIM].astype(BF16)
            s1 = _dot_nt(sk_ref[0, h], qh[:, 0:half])
            s2 = _dot_nt(sk_ref[1, h], qh[:, half:PEER_QDIM])
            t1 = _top_desc(s1, PEER_TOPK, io_k)
            t2 = jnp.concatenate(_top_desc(s2, PEER_TOPK, io_k), axis=0)
            cand = jnp.concatenate([t1[a] + t2[0:PEER_TOPK // (a + 1)] for a in range(PEER_TOPK)]
                                   + [jnp.full((n_cand_pad, tm), -jnp.inf, F32)], axis=0)
            ts = _top_desc(cand, PEER_TOPK, io_c)
            zsum = ts[0] * 0.0
            for a in range(PEER_TOPK):
                zsum = zsum + jnp.exp(ts[a] - ts[0])
            m1 = t1[0]
            m2 = t2[0:1]
            s1_sc[h] = s1
            s2_sc[h] = s2
            e1_sc[h] = jnp.exp(s1 - m1) / zsum
            e2_sc[h] = jnp.exp(s2 - m2)
            tau_sc[h] = jnp.broadcast_to(ts[PEER_TOPK - 1], (8, tm))

    act = jax.nn.gelu(_dot_nt(u_ref[...], hf_ref[...]))
    for j in range(PEER_I1_CHUNK):
        i1 = c * PEER_I1_CHUNK + j
        wj = jnp.zeros((PEER_KEYS, tm), F32)
        for h in range(PEER_HEADS):
            tot = s1_sc[h, pl.ds(i1, 1), :] + s2_sc[h]
            wj = wj + jnp.where(tot >= tau_sc[h, 0:1, :], e1_sc[h, pl.ds(i1, 1), :] * e2_sc[h], 0.0)
        gj = (wj * act[j * PEER_KEYS:(j + 1) * PEER_KEYS, :]).astype(BF16)
        acc_sc[...] += _dot(vt_ref[:, j * PEER_KEYS:(j + 1) * PEER_KEYS], gj)

    @pl.when(c == pl.num_programs(1) - 1)
    def _():
        o_ref[...] = x1_ref[...] + acc_sc[...].T


def _peer(hf_bf, qh, x1, subkeys, u_bf, vt_bf, tm):
    n = hf_bf.shape[0]
    n_exp = u_bf.shape[0]
    sk = subkeys.astype(BF16)
    nc = n_exp // PEER_ECHUNK
    return pl.pallas_call(
        _peer_kernel,
        grid=(n // tm, nc),
        in_specs=[pl.BlockSpec((tm, D_MODEL), lambda i, c: (i, 0)),
                  pl.BlockSpec((tm, PEER_HEADS * PEER_QDIM), lambda i, c: (i, 0)),
                  pl.BlockSpec(sk.shape, lambda i, c: (0, 0, 0, 0)),
                  pl.BlockSpec((PEER_ECHUNK, D_MODEL), lambda i, c: (c, 0)),
                  pl.BlockSpec((D_MODEL, PEER_ECHUNK), lambda i, c: (0, c)),
                  pl.BlockSpec((tm, D_MODEL), lambda i, c: (i, 0))],
        out_specs=pl.BlockSpec((tm, D_MODEL), lambda i, c: (i, 0)),
        out_shape=jax.ShapeDtypeStruct((n, D_MODEL), F32),
        scratch_shapes=[pltpu.VMEM((PEER_HEADS, PEER_KEYS, tm), F32)] * 4
        + [pltpu.VMEM((PEER_HEADS, 8, tm), F32), pltpu.VMEM((D_MODEL, tm), F32)],
        compiler_params=_cparams(2),
    )(hf_bf, qh, sk, u_bf, vt_bf, x1)


def _layer_prompt(x, lp, prep, rel_bias, lam_init):
    b, t, _ = x.shape
    n = b * t
    w_main, g64, g32, w_gate, u_bf, vt_bf = prep
    (mq, nq, dq, sq, sqi, misc, moba_new, nsa_new, win_new, diff_new, dsa_new, kmean, vt_moba, vt_diff, vt_s) = _in_proj(
        x.reshape(n, D_MODEL), lp['norm_mix'], w_main, g64, g32, TILE)
    r3 = lambda a: a.reshape(b, t, a.shape[-1])
    moba3, nsa3, win3, diff3, dsa3 = r3(moba_new), r3(nsa_new), r3(win_new), r3(diff_new), r3(dsa_new)
    bf = lambda a: a.astype(BF16)
    vt = lambda a: a.reshape(b, t // TILE, MIX_W, TILE)
    o_a = _moba_prompt(r3(mq), bf(moba3[..., :MIX_W]), vt(vt_moba),
                       kmean.reshape(b, t // TILE, MIX_W), rel_bias[:, 0:4])
    t16 = t // NSA_CMP_STRIDE * NSA_CMP_STRIDE
    cmp_kv = _nsa_compress(nsa3[:, :t16, 0:64], nsa3[:, :t16, 64:128], lp['nsa_pe'], lp['nsa_w1'], lp['nsa_w2'],
                           lp['qk_gain'][3])
    o_b = _nsa_prompt(r3(nq), r3(misc), cmp_kv, bf(nsa3[..., 128:192]), bf(win3[..., 0:64]), vt(vt_s), 0, 1,
                      rel_bias[:, 4:8])
    o_c = _diff_prompt(r3(dq), bf(diff3[..., :MIX_W]), vt(vt_diff), rel_bias[:, 8:12],
                       lp['diff_lambda'].astype(F32), lp['diff_out_gain'], lam_init)
    o_d = _dsa_prompt(r3(sq), r3(sqi), r3(misc), bf(dsa3[..., 0:64]), vt(vt_s), 2, bf(dsa3[..., 128:160]),
                      rel_bias[:, 12:16])
    f2 = lambda a: a.reshape(n, MIX_W)
    x1, hf_bf, qh = _merge(x.reshape(n, D_MODEL), lp['norm_mix'], w_gate, f2(o_a), f2(o_b), f2(o_c), f2(o_d),
                           lp['w_branch'], lp['w_out'], lp['norm_ffn'], lp['peer_wq'], TILE)
    x2 = _peer(hf_bf, qh, x1, lp['peer_subkeys'], u_bf, vt_bf, 512)
    return (x2.reshape(b, t, D_MODEL), moba3.reshape(b, t, 2, N_HEADS, HEAD_DIM), nsa3.reshape(b, t, 4, HEAD_DIM),
            diff3.reshape(b, t, 2, N_HEADS, HEAD_DIM), dsa3, win3.reshape(b, t, 2, HEAD_DIM))


ROWS = 8


def _fetch_pages(pt_ref, cache_ref, bufs, sems, layer, n_pages, page):
    b = pl.program_id(0)
    copies = []
    for j in range(n_pages):
        src = cache_ref.at[layer, pt_ref[b * n_pages + j]]
        for i, (buf, sem) in enumerate(zip(bufs, sems)):
            w = buf.shape[1]
            part = src if len(bufs) == 1 else src.at[:, pl.ds(i * w, w)]
            cp = pltpu.make_async_copy(part, buf.at[pl.ds(j * page, page), :], sem.at[j])
            cp.start()
            copies.append(cp)
    for cp in copies:
        cp.wait()


def _row_dot(qrows_bf, new_row):
    return jnp.sum(qrows_bf.astype(F32) * new_row.astype(BF16).astype(F32), axis=-1, keepdims=True)


def _softmax_av(s, s_new, v_bf, v_new):
    m = jnp.maximum(jnp.max(s, axis=-1, keepdims=True), s_new)
    p = jnp.exp(s - m)
    p_new = jnp.exp(s_new - m)
    den = jnp.sum(p, axis=-1, keepdims=True) + p_new
    num = _dot(p.astype(BF16), v_bf) + p_new.astype(BF16).astype(F32) * v_new.astype(BF16).astype(F32)
    return num / jnp.maximum(den, 1e-30)


def _top_lanes(score, jidx, n, width):
    sel = jnp.zeros(score.shape, F32)
    for _ in range(n):
        mx = jnp.max(score, axis=-1, keepdims=True)
        first = jnp.min(jnp.where(score == mx, jidx, width), axis=-1, keepdims=True)
        hit = jidx == first
        sel = jnp.where(hit, 1.0, sel)
        score = jnp.where(hit, -3.0 * BIG, score)
    return sel


def _moba_dec_kernel(pt_ref, q_ref, new_ref, bias_ref, ex_ref, cache_ref, o_ref, buf, sem, *, layer, n_pages, page):
    _fetch_pages(pt_ref, cache_ref, (buf,), (sem,), layer, n_pages, page)
    tk = n_pages * page
    nblk = tk // MOBA_BLOCK
    rows = lax.broadcasted_iota(jnp.int32, (ROWS, MIX_W), 0)
    lanes = lax.broadcasted_iota(jnp.int32, (ROWS, MIX_W), 1)
    hm = lanes // HEAD_DIM == rows
    qrows = jnp.where(hm, q_ref[0], 0.0).astype(BF16)
    kv = buf[...]
    kf = kv[:, 0:MIX_W]
    k = kf.astype(BF16)
    v = kv[:, MIX_W:2 * MIX_W].astype(BF16)
    kmean = jnp.mean(kf.reshape(nblk, MOBA_BLOCK, MIX_W), axis=1)
    kmean = jnp.concatenate([kmean, jnp.zeros((LANE - nblk, MIX_W), F32)], axis=0).astype(BF16)
    jidx = lax.broadcasted_iota(jnp.int32, (ROWS, LANE), 1)
    g = jnp.where(jidx < nblk, _dot_nt(qrows, kmean), -BIG)
    sel = jnp.zeros((ROWS, LANE), F32)
    for _ in range(min(MOBA_TOPK, nblk + 1)):
        mx = jnp.max(g, axis=-1, keepdims=True)
        first = jnp.min(jnp.where(g == mx, jidx, LANE), axis=-1, keepdims=True)
        hit = jidx == first
        sel = jnp.where(hit & (mx > -0.5 * BIG), 1.0, sel)
        g = jnp.where(hit, -3.0 * BIG, g)
    selk = _dot(sel.astype(BF16), ex_ref[...]) > 0.5
    scale = HEAD_DIM ** -0.5
    s = jnp.where(selk, _dot_nt(qrows, k) * scale + bias_ref[:, 0:tk], NEG)
    new = new_ref[0]
    s_new = _row_dot(qrows, new[:, 0:MIX_W]) * scale + bias_ref[:, tk:tk + 1]
    o = _softmax_av(s, s_new, v, new[:, MIX_W:2 * MIX_W])
    o_ref[0] = jnp.sum(jnp.where(hm, o, 0.0), axis=0, keepdims=True)


def _diff_dec_kernel(pt_ref, q_ref, new_ref, bias_ref, dl_ref, gain_ref, cache_ref, o_ref, buf, sem,
                     *, layer, n_pages, page, lam_init):
    _fetch_pages(pt_ref, cache_ref, (buf,), (sem,), layer, n_pages, page)
    tk = n_pages * page
    rows = lax.broadcasted_iota(jnp.int32, (ROWS, MIX_W), 0)
    lanes = lax.broadcasted_iota(jnp.int32, (ROWS, MIX_W), 1)
    qrows = jnp.where(lanes // DIFF_DH == rows, q_ref[0], 0.0).astype(BF16)
    kv = buf[...]
    k = kv[:, 0:MIX_W].astype(BF16)
    v = kv[:, MIX_W:2 * MIX_W].astype(BF16)
    scale = DIFF_DH ** -0.5
    s = _dot_nt(qrows, k) * scale + bias_ref[:, 0:tk]
    new = new_ref[0]
    s_new = _row_dot(qrows, new[:, 0:MIX_W]) * scale + bias_ref[:, tk:tk + 1]
    a = _softmax_av(s, s_new, v, new[:, MIX_W:2 * MIX_W])
    dl = dl_ref[...]
    lam = (jnp.exp(jnp.sum(dl[0:1] * dl[1:2], axis=-1, keepdims=True))
           - jnp.exp(jnp.sum(dl[2:3] * dl[3:4], axis=-1, keepdims=True)) + lam_init)
    sgn = jnp.where(rows % 2 == 0, 1.0, -lam)
    o = jnp.sum(jnp.where(lanes // HEAD_DIM == rows // 2, a * sgn, 0.0), axis=0, keepdims=True)
    for h in range(N_HEADS):
        oh = o[:, h * HEAD_DIM:(h + 1) * HEAD_DIM]
        o_ref[0, :, h * HEAD_DIM:(h + 1) * HEAD_DIM] = (
            oh * lax.rsqrt(jnp.mean(oh * oh, axis=-1, keepdims=True) + EPS) * gain_ref[...] * (1.0 - lam_init))


def _dsa_dec_kernel(pt_ref, q_ref, qi_ref, w_ref, new_ref, bias_ref, tri_ref, cache_ref, o_ref, buf, sem,
                    *, layer, n_pages, page):
    _fetch_pages(pt_ref, cache_ref, (buf,), (sem,), layer, n_pages, page)
    tk = n_pages * page
    kvi = buf[...]
    k = kvi[:, 0:HEAD_DIM].astype(BF16)
    v = kvi[:, HEAD_DIM:2 * HEAD_DIM].astype(BF16)
    ki = kvi[:, 2 * HEAD_DIM:DSA_ROW].astype(BF16)
    new = new_ref[0]
    qi = qi_ref[0].astype(BF16)
    w = w_ref[0]
    score = jnp.sum(jnp.maximum(_dot_nt(qi, ki), 0.0) * w, axis=0, keepdims=True)
    score_new = jnp.sum(jnp.maximum(_row_dot(qi, new[:, 2 * HEAD_DIM:DSA_ROW]), 0.0) * w, axis=0, keepdims=True)
    key = _sortable(score)
    key_new = _sortable(score_new)

    def count_ge(tau):
        return (jnp.sum(jnp.where(key >= tau, 1.0, 0.0), axis=-1, keepdims=True)
                + jnp.where(key_new >= tau, 1.0, 0.0))

    def bis_body(it, tau):
        cand = tau + lax.shift_left(jnp.int32(1), 31 - it)
        return jnp.where(count_ge(cand) >= float(DSA_TOPK), cand, tau)

    tau = lax.fori_loop(0, 32, bis_body, jnp.full((1, 1), INT_MIN, jnp.int32))
    need = float(DSA_TOPK) - count_ge(tau + 1)
    run = jnp.zeros((1, 1), F32)
    pieces = []
    for c in range(tk // LANE):
        kc = key[:, c * LANE:(c + 1) * LANE]
        eq = jnp.where(kc == tau, 1.0, 0.0)
        pref = run + _dot(jnp.broadcast_to(eq, (ROWS, LANE)).astype(BF16), tri_ref[...])[0:1]
        pieces.append(jnp.where((kc > tau) | ((kc == tau) & (pref <= need)), 1.0, 0.0))
        run = run + jnp.sum(eq, axis=-1, keepdims=True)
    mask = jnp.concatenate(pieces, axis=1) > 0.5
    sel_new = (key_new > tau) | ((key_new == tau) & (run + 1.0 <= need))
    q = q_ref[0].astype(BF16)
    scale = HEAD_DIM ** -0.5
    s = jnp.where(mask, _dot_nt(q, k) * scale + bias_ref[:, 0:tk], NEG)
    s_new = jnp.where(sel_new, _row_dot(q, new[:, 0:HEAD_DIM]) * scale + bias_ref[:, tk:tk + 1], NEG)
    o_ref[0] = _softmax_av(s, s_new, v, new[:, HEAD_DIM:2 * HEAD_DIM])


def _compress_math(c, i, pe_ref, w1_ref, w2_ref, g_ref):
    half = NSA_CMP_STRIDE * HEAD_DIM
    a = _dot((c + pe_ref[i, :, 0:half]).astype(BF16), w1_ref[i, 0:half, :])
    bm = _dot((c + pe_ref[i, :, half:2 * half]).astype(BF16), w1_ref[i, half:2 * half, :])
    hid = a + jnp.concatenate([bm[1:], jnp.zeros((1, bm.shape[1]), F32)], axis=0)
    y = _dot(jax.nn.gelu(hid).astype(BF16), w2_ref[i])
    if i == 0:
        y = y * lax.rsqrt(jnp.mean(y * y, axis=-1, keepdims=True) + EPS) * g_ref[...]
    return y


def _nsa_dec_kernel(pt_ref, q_ref, gl_ref, new_ref, win_ref, wnew_ref, pe_ref, w1_ref, w2_ref, kn_ref,
                    cb_ref, ov_ref, ex_ref, bias_ref, wb_ref, cache_ref, o_ref, buf, sem, buf_s, sem_s, c_sc,
                    *, layer, n_pages, page):
    _fetch_pages(pt_ref, cache_ref, (buf, buf_s), (sem, sem_s), layer, n_pages, page)
    tk = n_pages * page
    nch = tk // NSA_CMP_STRIDE
    n_win = win_ref.shape[2]
    scale = HEAD_DIM ** -0.5
    q = q_ref[0].astype(BF16)
    rowok = lax.broadcasted_iota(jnp.int32, (ROWS, 1), 0) < N_HEADS
    for l in range(NSA_CMP_STRIDE):
        rows = buf[pl.ds(l, nch, stride=NSA_CMP_STRIDE), :]
        c_sc[0, :, l * HEAD_DIM:(l + 1) * HEAD_DIM] = rows[:, 0:HEAD_DIM]
        c_sc[1, :, l * HEAD_DIM:(l + 1) * HEAD_DIM] = rows[:, HEAD_DIM:2 * HEAD_DIM]
    cmp = [_compress_math(c_sc[i], i, pe_ref, w1_ref, w2_ref, kn_ref).astype(BF16) for i in range(2)]
    cb = cb_ref[...]
    ok = cb > 0.5 * NEG
    z = jnp.where(ok, _dot_nt(q, cmp[0]) * scale + cb, NEG)
    z = z - jnp.max(z, axis=-1, keepdims=True)
    pc = jnp.where(ok & rowok, jnp.exp(z), 0.0)
    pc = (pc / jnp.maximum(jnp.sum(pc, axis=-1, keepdims=True), 1e-30)).astype(BF16)
    oc = _dot(pc, cmp[1])
    imp = jnp.sum(_dot(pc, ov_ref[...]), axis=0, keepdims=True)
    bt = tk // NSA_SLC_BLOCK
    jidx = lax.broadcasted_iota(jnp.int32, (1, LANE), 1)
    forced = (jidx == 0) | (jidx == bt) | (jidx == bt - 1)
    score = jnp.where(jidx <= bt, jnp.where(forced, BIG, imp), -BIG)
    sel = _top_lanes(score, jidx, min(NSA_N_SEL, bt + 1), LANE)
    sel_new = jnp.max(jnp.where(jidx == bt, sel, 0.0), axis=-1, keepdims=True) > 0.5
    selk = _dot(jnp.broadcast_to(sel, (ROWS, LANE)).astype(BF16), ex_ref[...]) > 0.5
    kv = buf_s[...]
    new = new_ref[0]
    s = jnp.where(selk, _dot_nt(q, kv[:, 0:HEAD_DIM].astype(BF16)) * scale + bias_ref[:, 0:tk], NEG)
    s_new = jnp.where(sel_new, _row_dot(q, new[:, 2 * HEAD_DIM:3 * HEAD_DIM]) * scale + bias_ref[:, tk:tk + 1], NEG)
    osel = _softmax_av(s, s_new, kv[:, HEAD_DIM:2 * HEAD_DIM].astype(BF16), new[:, 3 * HEAD_DIM:4 * HEAD_DIM])
    win = win_ref[0, 0]
    wnew = wnew_ref[0]
    sw = _dot_nt(q, win[:, 0:HEAD_DIM].astype(BF16)) * scale + wb_ref[:, 0:n_win]
    sw_new = _row_dot(q, wnew[:, 0:HEAD_DIM]) * scale + wb_ref[:, n_win:n_win + 1]
    ow = _softmax_av(sw, sw_new, win[:, HEAD_DIM:2 * HEAD_DIM].astype(BF16), wnew[:, HEAD_DIM:2 * HEAD_DIM])
    g = 1.0 / (1.0 + jnp.exp(-gl_ref[0]))
    o_ref[0] = g[:, 0:1] * oc + g[:, 1:2] * osel + g[:, 2:3] * ow


def _decode_call(body, page_table, cache, width, ins, in_specs, out_shape, out_spec, extra_scratch=()):
    b, n_pages = page_table.shape
    page = cache.shape[2]
    grid_spec = pltpu.PrefetchScalarGridSpec(
        num_scalar_prefetch=1, grid=(b,),
        in_specs=list(in_specs) + [pl.BlockSpec(memory_space=pl.ANY)],
        out_specs=out_spec,
        scratch_shapes=[pltpu.VMEM((n_pages * page, width), F32), pltpu.SemaphoreType.DMA((n_pages,))]
        + list(extra_scratch))
    return pl.pallas_call(body, grid_spec=grid_spec, out_shape=out_shape, compiler_params=_cparams(1))(
        page_table.reshape(-1), *ins, cache)


def _row_bias(tab_rows, pos, n_keys, kpos0=0, window=None):
    kpos = jnp.concatenate([kpos0 + jnp.arange(n_keys), jnp.full((1,), pos)])
    rel = pos - kpos
    bias = _bias_lookup(tab_rows.T, rel).T
    if window is not None:
        bias = jnp.where((rel < window)[None, :], bias, NEG)
    bias = jnp.pad(bias, ((0, ROWS - bias.shape[0]), (0, LANE - 1)))
    return bias.astype(F32)


def _pad_rows(a):
    return jnp.pad(a, ((0, 0), (0, ROWS - a.shape[1]), (0, 0)))


def _layer_sample(x, pos, caches, win_state, page_table, layer, lp, prep, rel_bias, lam_init):
    b = x.shape[0]
    n_pages = page_table.shape[1]
    page = caches[0].shape[2]
    tk = n_pages * page
    win_buf = win_state.shape[2]
    assert pos == tk and tk % MOBA_BLOCK == 0 and tk // MOBA_BLOCK <= LANE and tk // NSA_SLC_BLOCK < LANE
    assert tk // 4 >= DSA_TOPK and win_buf == NSA_WINDOW and tk // NSA_CMP_STRIDE == LANE
    w_main, g64, g32, w_gate, u_bf, vt_bf = prep
    (mq, nq, dq, sq, sqi, misc, moba_new, nsa_new, win_new, diff_new, dsa_new) = _in_proj(
        x.reshape(b, D_MODEL), lp['norm_mix'], w_main, g64, g32, b)[:11]
    depth, n_pool = caches[0].shape[0], caches[0].shape[1]
    c_moba = caches[0].reshape(depth, n_pool, page, 2 * MIX_W)
    c_nsa = caches[1].reshape(depth, n_pool, page, 4 * HEAD_DIM)
    c_diff = caches[2].reshape(depth, n_pool, page, 2 * MIX_W)
    c_dsa = caches[3]
    per_seq = lambda *shape: pl.BlockSpec((1,) + shape, lambda i, pt: (i,) + (0,) * len(shape))
    full = lambda a: pl.BlockSpec(a.shape, lambda i, pt: (0,) * a.ndim)
    tabs = [rel_bias[:, 4 * i:4 * i + 4].T for i in range(N_MIXERS)]
    kpos = jnp.arange(tk)
    kw = dict(layer=layer, n_pages=n_pages, page=page)

    bias = _row_bias(tabs[0], pos, tk)
    ex = (kpos[None, :] // MOBA_BLOCK == jnp.arange(LANE)[:, None]).astype(BF16)
    o_a = _decode_call(functools.partial(_moba_dec_kernel, **kw), page_table, c_moba, 2 * MIX_W,
                       (mq.reshape(b, 1, MIX_W), moba_new.reshape(b, 1, 2 * MIX_W), bias, ex),
                       [per_seq(1, MIX_W), per_seq(1, 2 * MIX_W), full(bias), full(ex)],
                       jax.ShapeDtypeStruct((b, 1, MIX_W), F32), per_seq(1, MIX_W))
    n_cmp = (tk + 1 - NSA_CMP_LEN) // NSA_CMP_STRIDE + 1
    cidx = jnp.arange(LANE)
    crel = pos - (cidx * NSA_CMP_STRIDE + NSA_CMP_LEN - 1)
    cbias = jnp.where(((crel >= 0) & (cidx < n_cmp))[None, :], _bias_lookup(tabs[1].T, crel).T, NEG)
    cbias = jnp.pad(cbias, ((0, ROWS - N_HEADS), (0, 0))).astype(F32)
    cstart = cidx * NSA_CMP_STRIDE
    sstart = jnp.arange(LANE) * NSA_SLC_BLOCK
    overlap = ((cstart[:, None] < sstart[None, :] + NSA_SLC_BLOCK) & (cstart[:, None] + NSA_CMP_LEN > sstart[None, :])
               & (cidx < n_cmp)[:, None] & (sstart[None, :] <= tk)).astype(BF16)
    ex_s = (kpos[None, :] // NSA_SLC_BLOCK == jnp.arange(LANE)[:, None]).astype(BF16)
    bias_s = _row_bias(tabs[1], pos, tk)
    wbias = _row_bias(tabs[1], pos, win_buf, kpos0=pos - win_buf, window=NSA_WINDOW)
    glog = _pad_rows(jnp.pad(misc[:, MISC_G:MISC_G + 3 * N_HEADS].reshape(b, N_HEADS, 3), ((0, 0), (0, 0), (0, LANE - 3))))
    pe2 = lp['nsa_pe'].reshape(2, 1, NSA_CMP_LEN * HEAD_DIM)
    w1b, w2b = lp['nsa_w1'].astype(BF16), lp['nsa_w2'].astype(BF16)
    kn = lp['qk_gain'][3].reshape(1, HEAD_DIM)
    win4 = win_state.reshape(win_state.shape[0], b, win_buf, 2 * HEAD_DIM)
    q4 = lambda a: _pad_rows(a.reshape(b, N_HEADS, HEAD_DIM))
    o_b = _decode_call(functools.partial(_nsa_dec_kernel, **kw), page_table, c_nsa, 2 * HEAD_DIM,
                       (q4(nq), glog, nsa_new.reshape(b, 1, 4 * HEAD_DIM), win4, win_new.reshape(b, 1, 2 * HEAD_DIM),
                        pe2, w1b, w2b, kn, cbias, overlap, ex_s, bias_s, wbias),
                       [per_seq(ROWS, HEAD_DIM), per_seq(ROWS, LANE), per_seq(1, 4 * HEAD_DIM),
                        pl.BlockSpec((1, 1, win_buf, 2 * HEAD_DIM), lambda i, pt: (layer, i, 0, 0)),
                        per_seq(1, 2 * HEAD_DIM), full(pe2), full(w1b), full(w2b), full(kn), full(cbias),
                        full(overlap), full(ex_s), full(bias_s), full(wbias)],
                       jax.ShapeDtypeStruct((b, ROWS, HEAD_DIM), F32), per_seq(ROWS, HEAD_DIM),
                       extra_scratch=[pltpu.VMEM((tk, 2 * HEAD_DIM), F32), pltpu.SemaphoreType.DMA((n_pages,)),
                                      pltpu.VMEM((2, tk // NSA_CMP_STRIDE, NSA_CMP_STRIDE * HEAD_DIM), F32)])
    bias_d = _row_bias(jnp.repeat(tabs[2], 2, axis=0), pos, tk)
    dl = lp['diff_lambda'].astype(F32)
    gain = lp['diff_out_gain'].reshape(1, HEAD_DIM)
    o_c = _decode_call(functools.partial(_diff_dec_kernel, lam_init=lam_init, **kw), page_table, c_diff, 2 * MIX_W,
                       (dq.reshape(b, 1, MIX_W), diff_new.reshape(b, 1, 2 * MIX_W), bias_d, dl, gain),
                       [per_seq(1, MIX_W), per_seq(1, 2 * MIX_W), full(bias_d), full(dl), full(gain)],
                       jax.ShapeDtypeStruct((b, 1, MIX_W), F32), per_seq(1, MIX_W))
    bias_i = _row_bias(tabs[3], pos, tk)
    i = np.arange(LANE)
    tri = jnp.asarray(i[:, None] <= i[None, :], BF16)
    qi8 = _pad_rows(sqi.reshape(b, DSA_IDX_HEADS, DSA_IDX_DIM))
    w8 = _pad_rows(misc[:, MISC_W:MISC_W + DSA_IDX_HEADS].reshape(b, DSA_IDX_HEADS, 1))
    o_d = _decode_call(functools.partial(_dsa_dec_kernel, **kw), page_table, c_dsa, DSA_ROW,
                       (q4(sq), qi8, w8, dsa_new.reshape(b, 1, DSA_ROW), bias_i, tri),
                       [per_seq(ROWS, HEAD_DIM), per_seq(ROWS, DSA_IDX_DIM), per_seq(ROWS, 1), per_seq(1, DSA_ROW),
                        full(bias_i), full(tri)],
                       jax.ShapeDtypeStruct((b, ROWS, HEAD_DIM), F32), per_seq(ROWS, HEAD_DIM))
    heads = lambda o: o[:, :N_HEADS].reshape(b, MIX_W)
    x1, hf_bf, qh = _merge(x.reshape(b, D_MODEL), lp['norm_mix'], w_gate, o_a.reshape(b, MIX_W), heads(o_b),
                           o_c.reshape(b, MIX_W), heads(o_d), lp['w_branch'], lp['w_out'], lp['norm_ffn'],
                           lp['peer_wq'], b)
    x2 = _peer(hf_bf, qh, x1, lp['peer_subkeys'], u_bf, vt_bf, b)
    wrows = jnp.concatenate([win_state[layer][:, 1:], win_new.reshape(b, 1, 2, HEAD_DIM)], axis=1)
    return (x2.reshape(b, 1, D_MODEL), moba_new.reshape(b, 1, 2, N_HEADS, HEAD_DIM), nsa_new.reshape(b, 1, 4, HEAD_DIM),
            diff_new.reshape(b, 1, 2, N_HEADS, HEAD_DIM), dsa_new.reshape(b, 1, DSA_ROW), wrows)


def kernel(x_prompt, x_sample, cache_moba, cache_nsa, cache_diff, cache_dsa, state_nsa_win, page_table,
           rel_bias, norm_mix, w_in, qk_gain, nsa_pe, nsa_w1, nsa_w2, diff_qk_gain, diff_lambda,
           diff_out_gain, w_branch, w_out, norm_ffn, peer_wq, peer_subkeys, peer_u, peer_v):
    depth = w_in.shape[0]
    past_len = page_table.shape[1] * cache_moba.shape[2]
    win_buf = state_nsa_win.shape[2]
    assert x_sample.shape[1] == 1 and x_prompt.shape[1] >= win_buf and past_len >= win_buf
    caches = (cache_moba, cache_nsa, cache_diff, cache_dsa)
    yp, ys = x_prompt, x_sample
    rows_p, rows_s = [], []
    for l in range(depth):
        lp = {'norm_mix': norm_mix[l], 'qk_gain': qk_gain[l], 'nsa_pe': nsa_pe[l],
              'nsa_w1': nsa_w1[l], 'nsa_w2': nsa_w2[l], 'diff_lambda': diff_lambda[l],
              'diff_out_gain': diff_out_gain[l], 'w_branch': w_branch[l],
              'w_out': w_out[l], 'norm_ffn': norm_ffn[l], 'peer_wq': peer_wq[l], 'peer_subkeys': peer_subkeys[l]}
        prep = _prep_w_in(w_in[l], qk_gain[l], diff_qk_gain[l]) + (peer_u[l].astype(BF16), _transpose_bf16(peer_v, l, 512))
        lam_init = 0.8 - 0.6 * math.exp(-0.3 * l)
        out_p = _layer_prompt(yp, lp, prep, rel_bias, lam_init)
        out_s = _layer_sample(ys, past_len, caches, state_nsa_win, page_table, l, lp, prep, rel_bias, lam_init)
        yp, ys = out_p[0], out_s[0]
        rows_p.append(out_p[1:5] + (out_p[5][:, -win_buf:],))
        rows_s.append(out_s[1:])

    def stk(rows, i):
        return jnp.stack([r[i] for r in rows], axis=0)

    return (yp, ys, stk(rows_p, 0), stk(rows_s, 0), stk(rows_p, 1), stk(rows_s, 1),
            stk(rows_p, 2), stk(rows_s, 2), stk(rows_p, 3), stk(rows_s, 3), stk(rows_p, 4), stk(rows_s, 4))
```
